```python
import math
import jax, jax.numpy as jnp
from jax import lax
import numpy as np

D_MODEL = 1024
BATCH = 8
SEQ = 8192
DEPTH = 1
DEC_BATCH = 8
DEC_SEQ = 2048
PAST_LEN = 128

GRID_W = 64
GDN_HEADS = 4
GDN_DK = 128
GDN_DV = 128
CONV_K = 5
CHUNK = 64
ATT_HEADS = 4
ATT_KV_HEADS = 2
ATT_HD = 128
ATT_GROUP = ATT_HEADS // ATT_KV_HEADS
Q_BLOCK = 128
ROPE_THETA = 10000.0
D_FF = 4 * D_MODEL
EPS = 1e-6

GDN_QK = GDN_HEADS * GDN_DK
GDN_VW = GDN_HEADS * GDN_DV
GDN_CONV_W = 2 * GDN_QK + GDN_VW
ATT_Q = ATT_HEADS * ATT_HD
ATT_KV = ATT_KV_HEADS * ATT_HD
MIX_WIDTH = GDN_VW + ATT_Q
IN_SIZES = (GDN_CONV_W, GDN_VW, GDN_HEADS, GDN_HEADS, GDN_HEADS, GDN_HEADS, ATT_Q, ATT_KV, ATT_KV)
IN_WIDTH = sum(IN_SIZES)

kernel_name = "hybrid_bidir_gdn_axial_gqa_encoder"


def _rmsnorm(x, w):
    xf = x.astype(jnp.float32)
    y = xf * lax.rsqrt(jnp.mean(xf * xf, axis=-1, keepdims=True) + EPS)
    return (y * w.astype(jnp.float32)).astype(x.dtype)


def _l2norm(x):
    return x * lax.rsqrt(jnp.sum(x * x, axis=-1, keepdims=True) + EPS)


def _split_cols(p):
    outs, off = [], 0
    for s in IN_SIZES:
        outs.append(p[..., off:off + s])
        off += s
    return outs


def _gated_delta_chunked(q, k, v, g, beta):
    B_, T, H, DK = q.shape
    DV = v.shape[-1]
    N = T // CHUNK

    def to_chunks(a):
        a = a.reshape((B_, N, CHUNK, H) + a.shape[3:])
        return jnp.moveaxis(a, 3, 1)

    qc, kc, vc = to_chunks(q), to_chunks(k), to_chunks(v)
    gc = jnp.cumsum(to_chunks(g), axis=-1)
    bc = to_chunks(beta)

    idx = jnp.arange(CHUNK)
    lower_incl = idx[:, None] >= idx[None, :]
    strict = idx[:, None] > idx[None, :]
    diff = gc[..., :, None] - gc[..., None, :]
    decay = jnp.where(lower_incl, jnp.exp(jnp.where(lower_incl, diff, 0.0)), 0.0)

    k_beta = kc * bc[..., None]
    v_beta = vc * bc[..., None]
    kk = jnp.einsum('bhnid,bhnjd->bhnij', k_beta, kc) * decay
    tri = jnp.eye(CHUNK, dtype=q.dtype) + jnp.where(strict, kk, 0.0)
    rhs = jnp.concatenate([v_beta, k_beta * jnp.exp(gc)[..., None]], axis=-1)
    sol = lax.linalg.triangular_solve(tri, rhs, left_side=True, lower=True, unit_diagonal=True)
    u = sol[..., :DV]
    w = sol[..., DV:]

    qk = jnp.einsum('bhnid,bhnjd->bhnij', qc, kc) * decay
    g_last = gc[..., -1]
    k_to_end = kc * jnp.exp(g_last[..., None] - gc)[..., None]
    q_dec = qc * jnp.exp(gc)[..., None]

    xs = tuple(jnp.moveaxis(a, 2, 0) for a in (q_dec, qk, u, w, k_to_end, g_last))

    def step(S, inp):
        q_d, a_qk, u_c, w_c, k_e, gl = inp
        v_new = u_c - jnp.einsum('bhcd,bhdv->bhcv', w_c, S)
        o = jnp.einsum('bhcd,bhdv->bhcv', q_d, S) + jnp.einsum('bhij,bhjv->bhiv', a_qk, v_new)
        S = S * jnp.exp(gl)[..., None, None] + jnp.einsum('bhcd,bhcv->bhdv', k_e, v_new)
        return S, o

    S0 = jnp.zeros((B_, H, DK, DV), q.dtype)
    _, o = lax.scan(step, S0, xs)
    o = jnp.moveaxis(o, 0, 2).reshape(B_, H, T, DV)
    return jnp.moveaxis(o, 1, 2)


def _gated_deltanet(qkv, z, b_f, b_b, a_f, a_b, conv_w, A_log_f, A_log_b, dt_bias_f, dt_bias_b, gdn_norm_w):
    B_, T, _ = qkv.shape
    qkv = lax.conv_general_dilated(qkv, conv_w[:, None, :].astype(qkv.dtype), window_strides=(1,),
                                   padding=[(CONV_K // 2, CONV_K // 2)],
                                   dimension_numbers=('NWC', 'WIO', 'NWC'),
                                   feature_group_count=GDN_CONV_W)
    qkv = jax.nn.silu(qkv).astype(jnp.float32)
    q = qkv[..., :GDN_QK].reshape(B_, T, GDN_HEADS, GDN_DK)
    k = qkv[..., GDN_QK:2 * GDN_QK].reshape(B_, T, GDN_HEADS, GDN_DK)
    v = qkv[..., 2 * GDN_QK:].reshape(B_, T, GDN_HEADS, GDN_DV)
    q = _l2norm(q) * (GDN_DK ** -0.5)
    k = _l2norm(k)

    def gates(b, a, A_log, dt_bias):
        beta = jax.nn.sigmoid(b.astype(jnp.float32))
        g = -jnp.exp(A_log.astype(jnp.float32)) * jax.nn.softplus(a.astype(jnp.float32) + dt_bias.astype(jnp.float32))
        return g, beta

    g_f, beta_f = gates(b_f, a_f, A_log_f, dt_bias_f)
    g_b, beta_b = gates(b_b, a_b, A_log_b, dt_bias_b)
    o_f = _gated_delta_chunked(q, k, v, g_f, beta_f)
    flip = lambda a: jnp.flip(a, axis=1)
    o_b = flip(_gated_delta_chunked(flip(q), flip(k), flip(v), flip(g_b), flip(beta_b)))
    o = o_f + o_b
    o = o * lax.rsqrt(jnp.mean(o * o, axis=-1, keepdims=True) + EPS) * gdn_norm_w.astype(jnp.float32)
    zz = z.astype(jnp.float32).reshape(B_, T, GDN_HEADS, GDN_DV)
    o = o * jax.nn.silu(zz)
    return o.reshape(B_, T, GDN_VW).astype(z.dtype)


def _rope_half(x, ang):
    m = ang.shape[-1]
    c = jnp.cos(ang)[:, None, :]
    s = jnp.sin(ang)[:, None, :]
    x1, x2 = x[..., :m], x[..., m:]
    return jnp.concatenate([x1 * c - x2 * s, x1 * s + x2 * c], axis=-1)


def _axial_rope(x, row_ang, col_ang):
    half = ATT_HD // 2
    xf = x.astype(jnp.float32)
    return jnp.concatenate([_rope_half(xf[..., :half], row_ang), _rope_half(xf[..., half:], col_ang)], axis=-1)


def _axial_gqa(qf, kf, vf, q_norm_w, k_norm_w):
    B_, T, _ = qf.shape
    rows = T // GRID_W
    row = jnp.repeat(jnp.arange(rows, dtype=jnp.float32), GRID_W)
    col = jnp.tile(jnp.arange(GRID_W, dtype=jnp.float32), rows)
    n_freq = ATT_HD // 4
    inv_freq = ROPE_THETA ** (-jnp.arange(n_freq, dtype=jnp.float32) / n_freq)
    row_ang = row[:, None] * inv_freq[None, :]
    col_ang = col[:, None] * inv_freq[None, :]

    q = _rmsnorm(qf.reshape(B_, T, ATT_HEADS, ATT_HD), q_norm_w)
    k = _rmsnorm(kf.reshape(B_, T, ATT_KV_HEADS, ATT_HD), k_norm_w)
    v = vf.reshape(B_, T, ATT_KV_HEADS, ATT_HD)
    q = _axial_rope(q, row_ang, col_ang).astype(vf.dtype)
    k = _axial_rope(k, row_ang, col_ang).astype(vf.dtype)
    scale = ATT_HD ** -0.5

    nb = T // Q_BLOCK
    qb = q.reshape(B_, nb, Q_BLOCK, ATT_KV_HEADS, ATT_GROUP, ATT_HD)
    qb = jnp.moveaxis(qb, 1, 0)

    def block(qi):
        s = jnp.einsum('bqkgd,btkd->bkgqt', qi, k).astype(jnp.float32) * scale
        p = jax.nn.softmax(s, axis=-1)
        return jnp.einsum('bkgqt,btkd->bqkgd', p.astype(v.dtype), v)

    o = lax.map(block, qb)
    return jnp.moveaxis(o, 0, 1).reshape(B_, T, ATT_Q)


def _layer(x, norm_mix_pre, w_in, conv_w, A_log_f, A_log_b, dt_bias_f, dt_bias_b, gdn_norm_w,
           q_norm_w, k_norm_w, w_out, norm_mix_post, norm_mlp_pre, w_up, w_down, norm_mlp_post):
    h = _rmsnorm(x, norm_mix_pre)
    p = h @ w_in
    qkv_a, z_a, b_f, b_b, a_f, a_b, q_b, k_b, v_b = _split_cols(p)
    o_a = _gated_deltanet(qkv_a, z_a, b_f, b_b, a_f, a_b, conv_w, A_log_f, A_log_b,
                          dt_bias_f, dt_bias_b, gdn_norm_w)
    o_b = _axial_gqa(q_b, k_b, v_b, q_norm_w, k_norm_w)
    mix = jnp.concatenate([o_a, o_b], axis=-1) @ w_out
    x = x + _rmsnorm(mix, norm_mix_post)
    hm = _rmsnorm(x, norm_mlp_pre)
    f = jnp.square(jax.nn.relu(hm @ w_up)) @ w_down
    return x + _rmsnorm(f, norm_mlp_post)


def setup_inputs(seed: int = 0) -> dict:
    key = jax.random.key(seed)
    ks = jax.random.split(key, 20)
    f32 = jnp.float32

    def gain(k, shape):
        return 1.0 + 0.1 * jax.random.normal(k, shape, f32)

    dt = jnp.exp(jax.random.uniform(ks[6], (DEPTH, 2, GDN_HEADS), f32, math.log(0.001), math.log(0.1)))
    dt_bias = dt + jnp.log(-jnp.expm1(-dt))
    A_log = jnp.log(jax.random.uniform(ks[7], (DEPTH, 2, GDN_HEADS), f32, 1.0, 16.0))
    return {
        "x_prompt": jax.random.normal(ks[0], (BATCH, SEQ, D_MODEL), f32),
        "x_sample": jax.random.normal(ks[1], (DEC_BATCH, DEC_SEQ, D_MODEL), f32),
        "norm_mix_pre": gain(ks[2], (DEPTH, D_MODEL)),
        "w_in": jax.random.normal(ks[3], (DEPTH, D_MODEL, IN_WIDTH), f32) * D_MODEL ** -0.5,
        "conv_w": jax.random.normal(ks[4], (DEPTH, CONV_K, GDN_CONV_W), f32) * CONV_K ** -0.5,
        "A_log_f": A_log[:, 0],
        "A_log_b": A_log[:, 1],
        "dt_bias_f": dt_bias[:, 0],
        "dt_bias_b": dt_bias[:, 1],
        "gdn_norm_w": gain(ks[8], (DEPTH, GDN_DV)),
        "q_norm_w": gain(ks[9], (DEPTH, ATT_HD)),
        "k_norm_w": gain(ks[10], (DEPTH, ATT_HD)),
        "w_out": jax.random.normal(ks[11], (DEPTH, MIX_WIDTH, D_MODEL), f32) * MIX_WIDTH ** -0.5,
        "norm_mix_post": gain(ks[12], (DEPTH, D_MODEL)),
        "norm_mlp_pre": gain(ks[13], (DEPTH, D_MODEL)),
        "w_up": jax.random.normal(ks[14], (DEPTH, D_MODEL, D_FF), f32) * D_MODEL ** -0.5,
        "w_down": jax.random.normal(ks[15], (DEPTH, D_FF, D_MODEL), f32) * D_FF ** -0.5,
        "norm_mlp_post": gain(ks[16], (DEPTH, D_MODEL)),
    }


def reference(x_prompt, x_sample, norm_mix_pre, w_in, conv_w, A_log_f, A_log_b, dt_bias_f, dt_bias_b,
              gdn_norm_w, q_norm_w, k_norm_w, w_out, norm_mix_post, norm_mlp_pre, w_up, w_down,
              norm_mlp_post):
    y_prompt = x_prompt
    y_sample = x_sample
    for l in range(DEPTH):
        args = (norm_mix_pre[l], w_in[l], conv_w[l], A_log_f[l], A_log_b[l], dt_bias_f[l], dt_bias_b[l],
                gdn_norm_w[l], q_norm_w[l], k_norm_w[l], w_out[l], norm_mix_post[l], norm_mlp_pre[l],
                w_up[l], w_down[l], norm_mlp_post[l])
        y_prompt = _layer(y_prompt, *args)
        y_sample = _layer(y_sample, *args)
    return (y_prompt, y_sample)
```

```python
import functools
import math

import jax
import jax.numpy as jnp
import numpy as np
from jax import lax
from jax.experimental import pallas as pl
from jax.experimental.pallas import tpu as pltpu

F32 = jnp.float32
BF16 = jnp.bfloat16

EPS = 1e-6
GRID_W = 64
ROPE_THETA = 10000.0
CONV_K = 5
HD = 128
GDN_HEADS = 4
ATT_HEADS = 4
ATT_KV_HEADS = 2
ATT_GROUP = ATT_HEADS // ATT_KV_HEADS
LANES = 128
CHUNK = LANES
GATE_ROWS = 8
VMEM_LIMIT = 56 * 1024 * 1024

_NT = (((1,), (1,)), ((), ()))


def _dot(a, b):
    return jnp.dot(a, b, preferred_element_type=F32)


def _dot_nt(a, b):
    return lax.dot_general(a, b, _NT, preferred_element_type=F32)


def _rms(x, w):
    return x * lax.rsqrt(jnp.mean(x * x, axis=-1, keepdims=True) + EPS) * w


def _sigmoid(x):
    return 1.0 / (1.0 + jnp.exp(-x))


def _in_proj_kernel(x_ref, nw_ref, wqkv_ref, wz_ref, wg_ref, wq_ref, wk_ref, wv_ref,
                    alog_ref, dtb_ref, cumf_ref, cumb_ref, cos_ref, sin_ref, qnw_ref, knw_ref,
                    qkv_ref, z_ref, gate_ref, q_ref, k_ref, v_ref, *, q_scale):
    x = x_ref[...]
    h = _rms(x, nw_ref[...]).astype(BF16)

    qkv_ref[...] = _dot(h, wqkv_ref[...]).astype(BF16)
    z_ref[...] = _dot(h, wz_ref[...]).astype(BF16)
    v_ref[...] = _dot(h, wv_ref[...]).astype(BF16)

    graw = _dot_nt(wg_ref[...], h)
    kind = lax.broadcasted_iota(jnp.int32, graw.shape, 0) % GATE_ROWS
    beta = _sigmoid(graw)
    a = graw + dtb_ref[...]
    softplus = jnp.maximum(a, 0.0) + jnp.log1p(jnp.exp(-jnp.abs(a)))
    g = -jnp.exp(alog_ref[...]) * softplus
    g = jnp.where(kind >= 2, g, 0.0)
    p1 = g.astype(BF16)
    r1 = g - p1.astype(F32)
    p2 = r1.astype(BF16)
    p3 = (r1 - p2.astype(F32)).astype(BF16)
    nrow = g.shape[0]
    parts = jnp.concatenate([p1, p2, p3], axis=0)
    cf = _dot(parts, cumf_ref[...])
    cb = _dot(parts, cumb_ref[...])
    cf = cf[:nrow] + cf[nrow:2 * nrow] + cf[2 * nrow:]
    cb = cb[:nrow] + cb[nrow:2 * nrow] + cb[2 * nrow:]
    gate_ref[...] = jnp.where(kind < 2, beta, jnp.where(kind == 2, cf, jnp.where(kind == 3, cb, 0.0)))

    cos = cos_ref[...]
    sin = sin_ref[...]

    def norm_rope(y, w, scale):
        y = _rms(y, w)
        y = y * cos + pltpu.roll(y, HD // 2, axis=1) * sin
        return (y * scale).astype(BF16) if scale != 1.0 else y.astype(BF16)

    qa = _dot(h, wq_ref[...])
    for hh in range(ATT_HEADS):
        q_ref[:, hh * HD:(hh + 1) * HD] = norm_rope(qa[:, hh * HD:(hh + 1) * HD], qnw_ref[...], q_scale)
    ka = _dot(h, wk_ref[...])
    for hh in range(ATT_KV_HEADS):
        k_ref[:, hh * HD:(hh + 1) * HD] = norm_rope(ka[:, hh * HD:(hh + 1) * HD], knw_ref[...], 1.0)


def _in_proj(x2, T, P, tm):
    n_tok, d = x2.shape
    nt = n_tok // tm
    tpb = T // tm
    const = lambda i: (0, 0)
    tile = lambda i: (i, 0)

    def full(a):
        return pl.BlockSpec(a.shape, const)

    in_specs = [
        pl.BlockSpec((tm, d), tile), full(P["nw_pre"]),
        full(P["w_qkv"]), full(P["w_z"]), full(P["w_g"]), full(P["w_q"]), full(P["w_k"]), full(P["w_v"]),
        full(P["alog_rows"]), full(P["dtb_rows"]), full(P["cum_f"]), full(P["cum_b"]),
        pl.BlockSpec((tm, HD), lambda i: (i % tpb, 0)), pl.BlockSpec((tm, HD), lambda i: (i % tpb, 0)),
        full(P["qnw"]), full(P["knw"]),
    ]
    n_gate = GDN_HEADS * GATE_ROWS
    out_shape = [
        jax.ShapeDtypeStruct((n_tok, 3 * GDN_HEADS * HD), BF16),
        jax.ShapeDtypeStruct((n_tok, GDN_HEADS * HD), BF16),
        jax.ShapeDtypeStruct((n_gate, n_tok), F32),
        jax.ShapeDtypeStruct((n_tok, ATT_HEADS * HD), BF16),
        jax.ShapeDtypeStruct((n_tok, ATT_KV_HEADS * HD), BF16),
        jax.ShapeDtypeStruct((n_tok, ATT_KV_HEADS * HD), BF16),
    ]
    out_specs = [
        pl.BlockSpec((tm, 3 * GDN_HEADS * HD), tile),
        pl.BlockSpec((tm, GDN_HEADS * HD), tile),
        pl.BlockSpec((n_gate, tm), lambda i: (0, i)),
        pl.BlockSpec((tm, ATT_HEADS * HD), tile),
        pl.BlockSpec((tm, ATT_KV_HEADS * HD), tile),
        pl.BlockSpec((tm, ATT_KV_HEADS * HD), tile),
    ]
    q_scale = (HD ** -0.5) * math.log2(math.e)
    cos, sin = P["rope"][T]
    return pl.pallas_call(
        functools.partial(_in_proj_kernel, q_scale=q_scale),
        grid=(nt,), in_specs=in_specs, out_specs=out_specs, out_shape=out_shape,
        compiler_params=pltpu.CompilerParams(dimension_semantics=("parallel",), vmem_limit_bytes=VMEM_LIMIT),
        name="in_proj",
    )(x2, P["nw_pre"], P["w_qkv"], P["w_z"], P["w_g"], P["w_q"], P["w_k"], P["w_v"],
      P["alog_rows"], P["dtb_rows"], P["cum_f"], P["cum_b"], cos, sin, P["qnw"], P["knw"])


CONV_ROWS = 256
CONV_HALO = 16


def _conv_silu(x_ref, cw, win_ref, t0, seq):
    r, hl = CONV_ROWS, CONV_HALO
    prev = x_ref[pl.ds(pl.multiple_of(jnp.maximum(t0 - hl, 0), hl), hl), :].astype(F32)
    nxt = x_ref[pl.ds(pl.multiple_of(jnp.minimum(t0 + r, seq - hl), hl), hl), :].astype(F32)
    win_ref[0:hl, :] = jnp.where(t0 > 0, prev, 0.0)
    win_ref[hl:hl + r, :] = x_ref[pl.ds(t0, r), :].astype(F32)
    win_ref[hl + r:hl + r + hl, :] = jnp.where(t0 + r < seq, nxt, 0.0)
    acc = jnp.zeros((r, LANES), F32)
    for j in range(CONV_K):
        off = hl - CONV_K // 2 + j
        acc = acc + cw[j:j + 1, :] * win_ref[off:off + r, :]
    return acc * _sigmoid(acc)


def _l2n(x):
    return x * lax.rsqrt(jnp.sum(x * x, axis=-1, keepdims=True) + EPS)


def _gdn_kernel(q_ref, k_ref, v_ref, z_ref, cwq_ref, cwk_ref, cwv_ref, gate_ref, nw_ref, o_ref,
                qn_ref, kn_ref, vn_ref, ob_ref, winq_ref, wink_ref, winv_ref, sf_ref, sb_ref, *, seq):
    n_chunk = seq // CHUNK
    c = CHUNK

    cwq, cwk, cwv = cwq_ref[0], cwk_ref[0], cwv_ref[0]

    def conv_body(i, carry):
        t0 = pl.multiple_of(i * CONV_ROWS, CONV_ROWS)
        q = _l2n(_conv_silu(q_ref, cwq, winq_ref, t0, seq)) * (HD ** -0.5)
        k = _l2n(_conv_silu(k_ref, cwk, wink_ref, t0, seq))
        v = _conv_silu(v_ref, cwv, winv_ref, t0, seq)
        qn_ref[pl.ds(t0, CONV_ROWS), :] = q.astype(BF16)
        kn_ref[pl.ds(t0, CONV_ROWS), :] = k.astype(BF16)
        vn_ref[pl.ds(t0, CONV_ROWS), :] = v.astype(BF16)
        return carry

    lax.fori_loop(0, seq // CONV_ROWS, conv_body, 0)

    ri = lax.broadcasted_iota(jnp.int32, (c, c), 0)
    ci = lax.broadcasted_iota(jnp.int32, (c, c), 1)
    eye = ri == ci
    eye_bf = jnp.where(eye, 1.0, 0.0).astype(BF16)

    def col(row):
        return jnp.sum(jnp.where(eye, row, 0.0), axis=1, keepdims=True)

    def inv_unit_tri(a):
        t = jnp.where(eye, 1.0, 0.0) - jnp.where((ri // 2) == (ci // 2), a, 0.0)
        b = 2
        while b < c:
            off = ((ri // (2 * b)) == (ci // (2 * b))) & ((ri // b) != (ci // b))
            tb = t.astype(BF16)
            t = t - _dot(_dot(tb, jnp.where(off, a, 0.0).astype(BF16)).astype(BF16), tb)
            b *= 2
        return t

    def direction(chunk, s_ref, first, backward):
        t0 = pl.multiple_of(chunk * c, c)
        q = qn_ref[pl.ds(t0, c), :]
        k = kn_ref[pl.ds(t0, c), :]
        v = vn_ref[pl.ds(t0, c), :]
        gates = gate_ref[:, pl.ds(t0, c)]
        beta_r = gates[1:2] if backward else gates[0:1]
        gc_r = gates[3:4] if backward else gates[2:3]
        gc_c = col(gc_r)
        beta_c = col(beta_r)
        mask = (ri <= ci) if backward else (ri >= ci)
        strict = (ri < ci) if backward else (ri > ci)
        e = jnp.where(mask, jnp.exp(jnp.where(mask, gc_c - gc_r, 0.0)), 0.0)
        gram = _dot_nt(k, k)
        a = jnp.where(strict, gram * e * beta_c, 0.0)
        t = inv_unit_tri(a)
        u = _dot((t * beta_r).astype(BF16), v)
        w = _dot((t * (beta_r * jnp.exp(gc_r))).astype(BF16), k)
        qk = (_dot_nt(q, k) * e).astype(BF16)
        q_dec = (q.astype(F32) * jnp.exp(gc_c)).astype(BF16)
        g_last = gc_r[:, 0:1] if backward else gc_r[:, c - 1:c]
        k_t = _dot_nt(eye_bf, k)
        ke_t = (k_t * jnp.exp(g_last - gc_r)).astype(BF16)

        s = s_ref[...]
        s_bf = s.astype(BF16)
        v_new = (u - _dot(w.astype(BF16), s_bf)).astype(BF16)
        o = _dot(q_dec, s_bf) + _dot(qk, v_new)
        s_ref[...] = s * jnp.exp(g_last) + _dot(ke_t, v_new)
        if first:
            ob_ref[pl.ds(t0, c), :] = o
        else:
            ob_ref[pl.ds(t0, c), :] += o

    sf_ref[...] = jnp.zeros_like(sf_ref)
    sb_ref[...] = jnp.zeros_like(sb_ref)

    def scan_body(first, i, carry):
        direction(i, sf_ref, first, False)
        direction(n_chunk - 1 - i, sb_ref, first, True)
        return carry

    half = n_chunk // 2
    lax.fori_loop(0, half, functools.partial(scan_body, True), 0)
    lax.fori_loop(half, n_chunk, functools.partial(scan_body, False), 0)

    nw = nw_ref[...]

    def norm_body(i, carry):
        t0 = pl.multiple_of(i * CONV_ROWS, CONV_ROWS)
        o = _rms(ob_ref[pl.ds(t0, CONV_ROWS), :], nw)
        zz = z_ref[pl.ds(t0, CONV_ROWS), :].astype(F32)
        o_ref[pl.ds(t0, CONV_ROWS), :] = (o * (zz * _sigmoid(zz))).astype(BF16)
        return carry

    lax.fori_loop(0, seq // CONV_ROWS, norm_body, 0)


def _gdn(qkv, z, gates, P, B, T):
    nh = GDN_HEADS
    single = pl.Buffered(1)
    in_specs = [
        pl.BlockSpec((None, T, HD), lambda b, h: (b, 0, h), pipeline_mode=single),
        pl.BlockSpec((None, T, HD), lambda b, h: (b, 0, nh + h), pipeline_mode=single),
        pl.BlockSpec((None, T, HD), lambda b, h: (b, 0, 2 * nh + h), pipeline_mode=single),
        pl.BlockSpec((None, T, HD), lambda b, h: (b, 0, h), pipeline_mode=single),
        pl.BlockSpec((1, 8, HD), lambda b, h: (h, 0, 0)),
        pl.BlockSpec((1, 8, HD), lambda b, h: (nh + h, 0, 0)),
        pl.BlockSpec((1, 8, HD), lambda b, h: (2 * nh + h, 0, 0)),
        pl.BlockSpec((GATE_ROWS, T), lambda b, h: (h, b), pipeline_mode=single),
        pl.BlockSpec((1, HD), lambda b, h: (0, 0)),
    ]
    scratch = [
        pltpu.VMEM((T, HD), BF16), pltpu.VMEM((T, HD), BF16), pltpu.VMEM((T, HD), BF16),
        pltpu.VMEM((T, HD), F32),
        pltpu.VMEM((CONV_ROWS + 2 * CONV_HALO, HD), F32),
        pltpu.VMEM((CONV_ROWS + 2 * CONV_HALO, HD), F32),
        pltpu.VMEM((CONV_ROWS + 2 * CONV_HALO, HD), F32),
        pltpu.VMEM((HD, HD), F32), pltpu.VMEM((HD, HD), F32),
    ]
    return pl.pallas_call(
        functools.partial(_gdn_kernel, seq=T),
        grid=(B, nh), in_specs=in_specs,
        out_specs=pl.BlockSpec((None, T, HD), lambda b, h: (b, 0, h)),
        out_shape=jax.ShapeDtypeStruct((B, T, nh * HD), BF16),
        scratch_shapes=scratch,
        compiler_params=pltpu.CompilerParams(dimension_semantics=("parallel", "parallel"),
                                             vmem_limit_bytes=VMEM_LIMIT),
        name="gdn",
    )(qkv, qkv, qkv, z, P["conv_w"], P["conv_w"], P["conv_w"], gates, P["gdn_nw"])


def _attn_kernel(q_ref, k_ref, v_ref, o_ref, m_ref, l_ref, acc_ref, *, seq, tk):
    tq = q_ref.shape[0]
    q = jnp.concatenate([q_ref[:, g * HD:(g + 1) * HD] for g in range(ATT_GROUP)], axis=0)
    m_ref[...] = jnp.full_like(m_ref, -jnp.inf)
    l_ref[...] = jnp.zeros_like(l_ref)
    acc_ref[...] = jnp.zeros_like(acc_ref)

    def body(j, carry):
        t0 = pl.multiple_of(j * tk, tk)
        s = _dot_nt(q, k_ref[pl.ds(t0, tk), :])
        m_old = m_ref[...]
        m_new = jnp.maximum(m_old, jnp.max(s, axis=-1, keepdims=True))
        p = jnp.exp2(s - m_new)
        alpha = jnp.exp2(m_old - m_new)
        l_ref[...] = alpha * l_ref[...] + jnp.sum(p, axis=-1, keepdims=True)
        acc_ref[...] = alpha * acc_ref[...] + _dot(p.astype(BF16), v_ref[pl.ds(t0, tk), :])
        m_ref[...] = m_new
        return carry

    lax.fori_loop(0, seq // tk, body, 0)
    out = acc_ref[...] / l_ref[...]
    for g in range(ATT_GROUP):
        o_ref[:, g * HD:(g + 1) * HD] = out[g * tq:(g + 1) * tq].astype(BF16)


def _attn(q, k, v, B, T, tq, tk):
    gw = ATT_GROUP * HD
    return pl.pallas_call(
        functools.partial(_attn_kernel, seq=T, tk=tk),
        grid=(B, ATT_KV_HEADS, T // tq),
        in_specs=[
            pl.BlockSpec((None, tq, gw), lambda b, h, i: (b, i, h)),
            pl.BlockSpec((None, T, HD), lambda b, h, i: (b, 0, h)),
            pl.BlockSpec((None, T, HD), lambda b, h, i: (b, 0, h)),
        ],
        out_specs=pl.BlockSpec((None, tq, gw), lambda b, h, i: (b, i, h)),
        out_shape=jax.ShapeDtypeStruct((B, T, ATT_HEADS * HD), BF16),
        scratch_shapes=[pltpu.VMEM((ATT_GROUP * tq, 1), F32), pltpu.VMEM((ATT_GROUP * tq, 1), F32),
                        pltpu.VMEM((ATT_GROUP * tq, HD), F32)],
        compiler_params=pltpu.CompilerParams(dimension_semantics=("parallel", "parallel", "parallel"),
                                             vmem_limit_bytes=VMEM_LIMIT),
        name="attn",
    )(q, k, v)


def _out_mlp_kernel(x_ref, oa_ref, ob_ref, woa_ref, wob_ref, npost_ref, npre2_ref, wup_ref, wdn_ref,
                    npost2_ref, y_ref, *, ff_chunk):
    mix = _dot(oa_ref[...], woa_ref[...]) + _dot(ob_ref[...], wob_ref[...])
    x1 = x_ref[...] + _rms(mix, npost_ref[...])
    hm = _rms(x1, npre2_ref[...]).astype(BF16)
    d_ff = wup_ref.shape[1]
    f = jnp.zeros(x1.shape, F32)
    for cc in range(d_ff // ff_chunk):
        hc = jnp.maximum(_dot(hm, wup_ref[:, cc * ff_chunk:(cc + 1) * ff_chunk]), 0.0)
        f = f + _dot((hc * hc).astype(BF16), wdn_ref[cc * ff_chunk:(cc + 1) * ff_chunk, :])
    y_ref[...] = x1 + _rms(f, npost2_ref[...])


def _out_mlp(x2, oa, ob, P, tm, ff_chunk):
    n_tok, d = x2.shape
    const = lambda i: (0, 0)
    tile = lambda i: (i, 0)
    single = pl.Buffered(1)

    def full(a):
        return pl.BlockSpec(a.shape, const, pipeline_mode=single)

    return pl.pallas_call(
        functools.partial(_out_mlp_kernel, ff_chunk=ff_chunk),
        grid=(n_tok // tm,),
        in_specs=[pl.BlockSpec((tm, d), tile), pl.BlockSpec((tm, oa.shape[1]), tile),
                  pl.BlockSpec((tm, ob.shape[1]), tile),
                  full(P["w_out_a"]), full(P["w_out_b"]), full(P["nw_post"]), full(P["nw_pre2"]),
                  full(P["w_up"]), full(P["w_down"]), full(P["nw_post2"])],
        out_specs=pl.BlockSpec((tm, d), tile),
        out_shape=jax.ShapeDtypeStruct((n_tok, d), F32),
        compiler_params=pltpu.CompilerParams(dimension_semantics=("parallel",), vmem_limit_bytes=VMEM_LIMIT),
        name="out_mlp",
    )(x2, oa, ob, P["w_out_a"], P["w_out_b"], P["nw_post"], P["nw_pre2"], P["w_up"], P["w_down"],
      P["nw_post2"])


def _rope_tables(T):
    n_freq = HD // 4
    t = np.arange(T)
    inv_freq = ROPE_THETA ** (-np.arange(n_freq, dtype=np.float32) / n_freq)
    row_ang = jnp.asarray((t // GRID_W).astype(np.float32))[:, None] * jnp.asarray(inv_freq)[None, :]
    col_ang = jnp.asarray((t % GRID_W).astype(np.float32))[:, None] * jnp.asarray(inv_freq)[None, :]
    cr, sr, cc, sc = jnp.cos(row_ang), jnp.sin(row_ang), jnp.cos(col_ang), jnp.sin(col_ang)
    return jnp.concatenate([cr, cc, cr, cc], axis=1), jnp.concatenate([-sr, -sc, sr, sc], axis=1)


def _chunk_cumsum_mats(tm):
    t = np.arange(tm)
    same = (t[:, None] // CHUNK) == (t[None, :] // CHUNK)
    fwd = same & (t[:, None] <= t[None, :])
    bwd = same & (t[:, None] >= t[None, :])
    return jnp.asarray(fwd, BF16), jnp.asarray(bwd, BF16)


def _prepare(norm_mix_pre, w_in, conv_w, A_log_f, A_log_b, dt_bias_f, dt_bias_b, gdn_norm_w, q_norm_w,
             k_norm_w, w_out, norm_mix_post, norm_mlp_pre, w_up, w_down, norm_mlp_post, seqs, tm):
    gw = GDN_HEADS * HD
    sizes = (3 * gw, gw, GDN_HEADS, GDN_HEADS, GDN_HEADS, GDN_HEADS, ATT_HEADS * HD, ATT_KV_HEADS * HD,
             ATT_KV_HEADS * HD)
    offs = np.concatenate([[0], np.cumsum(sizes)])
    cols = [w_in[:, offs[i]:offs[i + 1]] for i in range(len(sizes))]
    w_qkv, w_z, w_bf, w_bb, w_af, w_ab, w_q, w_k, w_v = cols
    d = w_in.shape[0]
    perm = np.concatenate([np.arange(0, 32), np.arange(64, 96), np.arange(32, 64), np.arange(96, 128)])

    def permute_heads(w, n):
        return w.reshape(d, n, HD)[:, :, perm].reshape(d, n * HD)

    zeros = jnp.zeros((d, GDN_HEADS), F32)
    w_g = jnp.stack([w_bf, w_bb, w_af, w_ab, zeros, zeros, zeros, zeros], axis=2)
    w_g = w_g.reshape(d, GDN_HEADS * GATE_ROWS).T
    zh = jnp.zeros((GDN_HEADS,), F32)
    alog_rows = jnp.stack([zh, zh, A_log_f, A_log_b, zh, zh, zh, zh], axis=1).reshape(-1, 1)
    dtb_rows = jnp.stack([zh, zh, dt_bias_f, dt_bias_b, zh, zh, zh, zh], axis=1).reshape(-1, 1)
    cum_f, cum_b = _chunk_cumsum_mats(tm)
    cw = jnp.pad(conv_w, ((0, 8 - CONV_K), (0, 0)))
    cw = cw.reshape(8, 3 * GDN_HEADS, HD).transpose(1, 0, 2)
    row = lambda a: a.reshape(1, -1).astype(F32)
    return {
        "nw_pre": row(norm_mix_pre),
        "w_qkv": w_qkv.astype(BF16), "w_z": w_z.astype(BF16), "w_g": w_g.astype(BF16),
        "w_q": permute_heads(w_q, ATT_HEADS).astype(BF16),
        "w_k": permute_heads(w_k, ATT_KV_HEADS).astype(BF16),
        "w_v": w_v.astype(BF16),
        "alog_rows": alog_rows, "dtb_rows": dtb_rows, "cum_f": cum_f, "cum_b": cum_b,
        "rope": {T: _rope_tables(T) for T in seqs},
        "qnw": row(q_norm_w[perm]), "knw": row(k_norm_w[perm]),
        "conv_w": cw, "gdn_nw": row(gdn_norm_w),
        "w_out_a": w_out[:gw].astype(BF16), "w_out_b": w_out[gw:].astype(BF16),
        "nw_post": row(norm_mix_post), "nw_pre2": row(norm_mlp_pre),
        "w_up": w_up.astype(BF16), "w_down": w_down.astype(BF16), "nw_post2": row(norm_mlp_post),
    }


def _layer(x, P, tm, tq, tk, ff_chunk):
    B, T, d = x.shape
    x2 = x.reshape(B * T, d)
    qkv, z, gates, q, k, v = _in_proj(x2, T, P, tm)
    o_a = _gdn(qkv.reshape(B, T, -1), z.reshape(B, T, -1), gates, P, B, T)
    o_b = _attn(q.reshape(B, T, -1), k.reshape(B, T, -1), v.reshape(B, T, -1), B, T, tq, min(tk, T))
    y = _out_mlp(x2, o_a.reshape(B * T, -1), o_b.reshape(B * T, -1), P, tm, ff_chunk)
    return y.reshape(B, T, d)


TOKEN_TILE = 512
ATT_Q_TILE = 256
ATT_K_TILE = 512
FF_CHUNK = 1024


def kernel(x_prompt, x_sample, norm_mix_pre, w_in, conv_w, A_log_f, A_log_b, dt_bias_f, dt_bias_b,
           gdn_norm_w, q_norm_w, k_norm_w, w_out, norm_mix_post, norm_mlp_pre, w_up, w_down,
           norm_mlp_post):
    y_prompt, y_sample = x_prompt, x_sample
    for l in range(w_in.shape[0]):
        P = _prepare(norm_mix_pre[l], w_in[l], conv_w[l], A_log_f[l], A_log_b[l], dt_bias_f[l],
                     dt_bias_b[l], gdn_norm_w[l], q_norm_w[l], k_norm_w[l], w_out[l], norm_mix_post[l],
                     norm_mlp_pre[l], w_up[l], w_down[l], norm_mlp_post[l],
                     seqs={y_prompt.shape[1], y_sample.shape[1]}, tm=TOKEN_TILE)
        y_prompt = _layer(y_prompt, P, TOKEN_TILE, ATT_Q_TILE, ATT_K_TILE, FF_CHUNK)
        y_sample = _layer(y_sample, P, TOKEN_TILE, ATT_Q_TILE, ATT_K_TILE, FF_CHUNK)
    return (y_prompt, y_sample)
```

```python
import functools
import math

import jax
import jax.numpy as jnp
import numpy as np
from jax import lax
from jax.experimental import pallas as pl
from jax.experimental.pallas import tpu as pltpu

F32 = jnp.float32
BF16 = jnp.bfloat16

EPS = 1e-6
GRID_W = 64
ROPE_THETA = 10000.0
CONV_K = 5
HD = 128
GDN_HEADS = 4
ATT_HEADS = 4
ATT_KV_HEADS = 2
ATT_GROUP = ATT_HEADS // ATT_KV_HEADS
LANES = 128
CHUNK = LANES
GATE_ROWS = 8
VMEM_LIMIT = 56 * 1024 * 1024
IN_SUB = 256
ATT_ONES_ROWS = 16

_NT = (((1,), (1,)), ((), ()))


def _dot(a, b):
    return jnp.dot(a, b, preferred_element_type=F32)


def _dot_nt(a, b):
    return lax.dot_general(a, b, _NT, preferred_element_type=F32)


def _rms(x, w):
    return x * lax.rsqrt(jnp.mean(x * x, axis=-1, keepdims=True) + EPS) * w


def _sigmoid(x):
    return 1.0 / (1.0 + jnp.exp(-x))


def _in_proj_kernel(x_ref, nw_ref, wqkv_ref, wz_ref, wg_ref, wq_ref, wk_ref, wv_ref,
                    alog_ref, dtb_ref, cumf_ref, cumb_ref, cos_ref, sin_ref, qnw_ref, knw_ref,
                    qkv_ref, z_ref, gate_ref, q_ref, k_ref, vt_ref, *, q_scale):
    for s in range(x_ref.shape[0] // IN_SUB):
        r = slice(s * IN_SUB, (s + 1) * IN_SUB)
        h = _rms(x_ref[r, :], nw_ref[...]).astype(BF16)

        qkv_ref[r, :] = _dot(h, wqkv_ref[...]).astype(BF16)
        z_ref[r, :] = _dot(h, wz_ref[...]).astype(BF16)
        vt_ref[:, r] = _dot_nt(wv_ref[...], h).astype(BF16)

        graw = _dot_nt(wg_ref[...], h)
        kind = lax.broadcasted_iota(jnp.int32, graw.shape, 0) % GATE_ROWS
        beta = _sigmoid(graw)
        a = graw + dtb_ref[...]
        softplus = jnp.maximum(a, 0.0) + jnp.log1p(jnp.exp(-jnp.abs(a)))
        g = -jnp.exp(alog_ref[...]) * softplus
        g = jnp.where(kind >= 2, g, 0.0)
        p1 = g.astype(BF16)
        r1 = g - p1.astype(F32)
        p2 = r1.astype(BF16)
        p3 = (r1 - p2.astype(F32)).astype(BF16)
        nrow = g.shape[0]
        parts = jnp.concatenate([p1, p2, p3], axis=0)
        cf = _dot(parts, cumf_ref[...])
        cb = _dot(parts, cumb_ref[...])
        cf = cf[:nrow] + cf[nrow:2 * nrow] + cf[2 * nrow:]
        cb = cb[:nrow] + cb[nrow:2 * nrow] + cb[2 * nrow:]
        gate_ref[:, r] = jnp.where(kind < 2, beta, jnp.where(kind == 2, cf, jnp.where(kind == 3, cb, 0.0)))

        cos = cos_ref[r, :]
        sin = sin_ref[r, :]

        def norm_rope(y, w, scale):
            y = _rms(y, w)
            y = y * cos + pltpu.roll(y, HD // 2, axis=1) * sin
            return (y * scale).astype(BF16) if scale != 1.0 else y.astype(BF16)

        qa = _dot(h, wq_ref[...])
        for hh in range(ATT_HEADS):
            q_ref[r, hh * HD:(hh + 1) * HD] = norm_rope(qa[:, hh * HD:(hh + 1) * HD], qnw_ref[...], q_scale)
        ka = _dot(h, wk_ref[...])
        for hh in range(ATT_KV_HEADS):
            k_ref[r, hh * HD:(hh + 1) * HD] = norm_rope(ka[:, hh * HD:(hh + 1) * HD], knw_ref[...], 1.0)


def _in_proj(x2, T, P, tm):
    n_tok, d = x2.shape
    nt = n_tok // tm
    tpb = T // tm
    const = lambda i: (0, 0)
    tile = lambda i: (i, 0)

    def full(a):
        return pl.BlockSpec(a.shape, const)

    in_specs = [
        pl.BlockSpec((tm, d), tile), full(P["nw_pre"]),
        full(P["w_qkv"]), full(P["w_z"]), full(P["w_g"]), full(P["w_q"]), full(P["w_k"]), full(P["w_v"]),
        full(P["alog_rows"]), full(P["dtb_rows"]), full(P["cum_f"]), full(P["cum_b"]),
        pl.BlockSpec((tm, HD), lambda i: (i % tpb, 0)), pl.BlockSpec((tm, HD), lambda i: (i % tpb, 0)),
        full(P["qnw"]), full(P["knw"]),
    ]
    n_gate = GDN_HEADS * GATE_ROWS
    out_shape = [
        jax.ShapeDtypeStruct((n_tok, 3 * GDN_HEADS * HD), BF16),
        jax.ShapeDtypeStruct((n_tok, GDN_HEADS * HD), BF16),
        jax.ShapeDtypeStruct((n_gate, n_tok), F32),
        jax.ShapeDtypeStruct((n_tok, ATT_HEADS * HD), BF16),
        jax.ShapeDtypeStruct((n_tok, ATT_KV_HEADS * HD), BF16),
        jax.ShapeDtypeStruct((ATT_KV_HEADS * HD, n_tok), BF16),
    ]
    out_specs = [
        pl.BlockSpec((tm, 3 * GDN_HEADS * HD), tile),
        pl.BlockSpec((tm, GDN_HEADS * HD), tile),
        pl.BlockSpec((n_gate, tm), lambda i: (0, i)),
        pl.BlockSpec((tm, ATT_HEADS * HD), tile),
        pl.BlockSpec((tm, ATT_KV_HEADS * HD), tile),
        pl.BlockSpec((ATT_KV_HEADS * HD, tm), lambda i: (0, i)),
    ]
    q_scale = (HD ** -0.5) * math.log2(math.e)
    cos, sin = P["rope"][T]
    return pl.pallas_call(
        functools.partial(_in_proj_kernel, q_scale=q_scale),
        grid=(nt,), in_specs=in_specs, out_specs=out_specs, out_shape=out_shape,
        compiler_params=pltpu.CompilerParams(dimension_semantics=("parallel",), vmem_limit_bytes=VMEM_LIMIT),
        name="in_proj",
    )(x2, P["nw_pre"], P["w_qkv"], P["w_z"], P["w_g"], P["w_q"], P["w_k"], P["w_v"],
      P["alog_rows"], P["dtb_rows"], P["cum_f"], P["cum_b"], cos, sin, P["qnw"], P["knw"])


CONV_ROWS = 256
CONV_HALO = 16
PREP_CHUNKS = 4


def _conv_silu(x_ref, cw, win_ref, t0, seq):
    r, hl = CONV_ROWS, CONV_HALO
    prev = x_ref[pl.ds(pl.multiple_of(jnp.maximum(t0 - hl, 0), hl), hl), :].astype(F32)
    nxt = x_ref[pl.ds(pl.multiple_of(jnp.minimum(t0 + r, seq - hl), hl), hl), :].astype(F32)
    win_ref[0:hl, :] = jnp.where(t0 > 0, prev, 0.0)
    win_ref[hl:hl + r, :] = x_ref[pl.ds(t0, r), :].astype(F32)
    win_ref[hl + r:hl + r + hl, :] = jnp.where(t0 + r < seq, nxt, 0.0)
    acc = jnp.zeros((r, LANES), F32)
    for j in range(CONV_K):
        off = hl - CONV_K // 2 + j
        acc = acc + cw[j:j + 1, :] * win_ref[off:off + r, :]
    return acc * _sigmoid(acc)


def _l2n(x):
    return x * lax.rsqrt(jnp.sum(x * x, axis=-1, keepdims=True) + EPS)


def _gdn_kernel(q_ref, k_ref, v_ref, z_ref, cwq_ref, cwk_ref, cwv_ref, gate_ref, nw_ref, o_ref,
                qn_ref, kn_ref, vn_ref, ob_ref, winq_ref, wink_ref, winv_ref,
                mqf_ref, nf_ref, decf_ref, mqb_ref, nb_ref, decb_ref, *, seq):
    n_chunk = seq // CHUNK
    c = CHUNK
    dir_refs = ((mqf_ref, nf_ref, decf_ref), (mqb_ref, nb_ref, decb_ref))

    cwq, cwk, cwv = cwq_ref[0], cwk_ref[0], cwv_ref[0]

    def conv_body(i, carry):
        t0 = pl.multiple_of(i * CONV_ROWS, CONV_ROWS)
        q = _l2n(_conv_silu(q_ref, cwq, winq_ref, t0, seq)) * (HD ** -0.5)
        k = _l2n(_conv_silu(k_ref, cwk, wink_ref, t0, seq))
        v = _conv_silu(v_ref, cwv, winv_ref, t0, seq)
        qn_ref[pl.ds(t0, CONV_ROWS), :] = q.astype(BF16)
        kn_ref[pl.ds(t0, CONV_ROWS), :] = k.astype(BF16)
        vn_ref[pl.ds(t0, CONV_ROWS), :] = v.astype(BF16)
        return carry

    lax.fori_loop(0, seq // CONV_ROWS, conv_body, 0)

    ri =lax.broadcasted_iota(jnp.int32, (c, c), 0)
    ci = lax.broadcasted_iota(jnp.int32, (c, c), 1)
    eye = ri == ci
    eye_bf = jnp.where(eye, 1.0, 0.0).astype(BF16)

    def col(row):
        return jnp.sum(jnp.where(eye, row, 0.0), axis=1, keepdims=True)

    def inv_unit_tri(mats):
        ts = [jnp.where(eye, 1.0, 0.0) - jnp.where((ri // 2) == (ci // 2), a, 0.0) for a in mats]
        b = 2
        while b < c:
            off = ((ri // (2 * b)) == (ci // (2 * b))) & ((ri // b) != (ci // b))
            tbs = [t.astype(BF16) for t in ts]
            xs = [_dot(jnp.where(off, a, 0.0).astype(BF16), tb).astype(BF16) for a, tb in zip(mats, tbs)]
            ts = [t - _dot(tb, x) for t, tb, x in zip(ts, tbs, xs)]
            b *= 2
        return ts

    def prep_group(chunks):
        t0s = [pl.multiple_of(ch * c, c) for ch in chunks]
        qs = [qn_ref[pl.ds(t0, c), :] for t0 in t0s]
        ks = [kn_ref[pl.ds(t0, c), :] for t0 in t0s]
        vs = [vn_ref[pl.ds(t0, c), :] for t0 in t0s]
        gts = [gate_ref[:, pl.ds(t0, c)] for t0 in t0s]
        grams = [_dot_nt(k, k) for k in ks]
        qks = [_dot_nt(q, k) for q, k in zip(qs, ks)]
        kts = [_dot_nt(eye_bf, k) for k in ks]
        probs = [(n, d) for n in range(len(chunks)) for d in range(2)]
        beta_r = [gts[n][d:d + 1] for n, d in probs]
        gc_r = [gts[n][2 + d:3 + d] for n, d in probs]
        gc_c = [col(g) for g in gc_r]
        beta_c = [col(bb) for bb in beta_r]
        masks = [(ri <= ci) if d else (ri >= ci) for n, d in probs]
        stricts = [(ri < ci) if d else (ri > ci) for n, d in probs]
        es = [jnp.where(m, jnp.exp(jnp.where(m, gcc - gcr, 0.0)), 0.0) for m, gcc, gcr in zip(masks, gc_c, gc_r)]
        amats = [jnp.where(st, grams[n] * e * bc, 0.0) for (n, d), st, e, bc in zip(probs, stricts, es, beta_c)]
        ts = inv_unit_tri(amats)
        us = [_dot((t * br).astype(BF16), vs[n]) for (n, d), t, br in zip(probs, ts, beta_r)]
        ws = [_dot((t * (br * jnp.exp(gr))).astype(BF16), ks[n]) for (n, d), t, br, gr in zip(probs, ts, beta_r, gc_r)]
        wus = [jnp.concatenate([w, u], axis=1).astype(BF16) for w, u in zip(ws, us)]
        g_last = [gr[:, 0:1] if d else gr[:, c - 1:c] for (n, d), gr in zip(probs, gc_r)]
        kes = [(kts[n] * jnp.exp(gl - gr)).astype(BF16) for (n, d), gl, gr in zip(probs, g_last, gc_r)]
        qkms = [(qks[n] * e).astype(BF16) for (n, d), e in zip(probs, es)]
        mns = [_dot(ke, wu) for ke, wu in zip(kes, wus)]
        qos = [_dot(qkm, wu) for qkm, wu in zip(qkms, wus)]
        for i, (n, d) in enumerate(probs):
            mq_ref, n_ref, dec_ref = dir_refs[d]
            m0 = pl.multiple_of(chunks[n] * (2 * c), 2 * c)
            mq_ref[pl.ds(m0, c), :] = mns[i][:, :HD].astype(BF16)
            q_dec = qs[n].astype(F32) * jnp.exp(gc_c[i])
            mq_ref[pl.ds(m0 + c, c), :] = (q_dec - qos[i][:, :HD]).astype(BF16)
            n_ref[pl.ds(t0s[n], c), :] = mns[i][:, HD:]
            dec_ref[pl.ds(pl.multiple_of(chunks[n] * 8, 8), 8), :] = jnp.broadcast_to(jnp.exp(g_last[i]), (8, HD))
        for n in range(len(chunks)):
            ob_ref[pl.ds(t0s[n], c), :] = qos[2 * n][:, HD:] + qos[2 * n + 1][:, HD:]

    def prep_body(i, carry):
        prep_group([i * PREP_CHUNKS + cc for cc in range(PREP_CHUNKS)])
        return carry

    lax.fori_loop(0, n_chunk // PREP_CHUNKS, prep_body, 0)

    def step(chunk, s, refs):
        mq_ref, n_ref, dec_ref = refs
        t0 = pl.multiple_of(chunk * c, c)
        x = _dot(mq_ref[pl.ds(pl.multiple_of(chunk * (2 * c), 2 * c), 2 * c), :], s.astype(BF16))
        dec = dec_ref[pl.ds(pl.multiple_of(chunk * 8, 8), 8), :][0:1]
        ob_ref[pl.ds(t0, c), :] += x[c:]
        return s * dec - x[:c] + n_ref[pl.ds(t0, c), :]

    def scan_body(i, carry):
        s_f, s_b = carry
        return (step(i, s_f, dir_refs[0]), step(n_chunk - 1 - i, s_b, dir_refs[1]))

    s0 = jnp.zeros((HD, HD), F32)
    lax.fori_loop(0, n_chunk, scan_body, (s0, s0))

    nw = nw_ref[...]

    def norm_body(i, carry):
        t0 = pl.multiple_of(i * CONV_ROWS, CONV_ROWS)
        o = _rms(ob_ref[pl.ds(t0, CONV_ROWS), :], nw)
        zz = z_ref[pl.ds(t0, CONV_ROWS), :].astype(F32)
        o_ref[pl.ds(t0, CONV_ROWS), :] = (o * (zz * _sigmoid(zz))).astype(BF16)
        return carry

    lax.fori_loop(0, seq // CONV_ROWS, norm_body, 0)


def _gdn(qkv, z, gates, P, B, T):
    nh = GDN_HEADS
    single = pl.Buffered(1)
    in_specs = [
        pl.BlockSpec((None, T, HD), lambda b, h: (b, 0, h), pipeline_mode=single),
        pl.BlockSpec((None, T, HD), lambda b, h: (b, 0, nh + h), pipeline_mode=single),
        pl.BlockSpec((None, T, HD), lambda b, h: (b, 0, 2 * nh + h), pipeline_mode=single),
        pl.BlockSpec((None, T, HD), lambda b, h: (b, 0, h), pipeline_mode=single),
        pl.BlockSpec((1, 8, HD), lambda b, h: (h, 0, 0)),
        pl.BlockSpec((1, 8, HD), lambda b, h: (nh + h, 0, 0)),
        pl.BlockSpec((1, 8, HD), lambda b, h: (2 * nh + h, 0, 0)),
        pl.BlockSpec((GATE_ROWS, T), lambda b, h: (h, b), pipeline_mode=single),
        pl.BlockSpec((1, HD), lambda b, h: (0, 0)),
    ]
    scratch = [
        pltpu.VMEM((T, HD), BF16), pltpu.VMEM((T, HD), BF16), pltpu.VMEM((T, HD), BF16),
        pltpu.VMEM((T, HD), F32),
        pltpu.VMEM((CONV_ROWS + 2 * CONV_HALO, HD), F32),
        pltpu.VMEM((CONV_ROWS + 2 * CONV_HALO, HD), F32),
        pltpu.VMEM((CONV_ROWS + 2 * CONV_HALO, HD), F32),
    ]
    per_direction = [pltpu.VMEM((2 * T, HD), BF16), pltpu.VMEM((T, HD), F32),
                     pltpu.VMEM((T // CHUNK * 8, HD), F32)]
    scratch = scratch + per_direction + per_direction
    return pl.pallas_call(
        functools.partial(_gdn_kernel, seq=T),
        grid=(B, nh), in_specs=in_specs,
        out_specs=pl.BlockSpec((None, T, HD), lambda b, h: (b, 0, h)),
        out_shape=jax.ShapeDtypeStruct((B, T, nh * HD), BF16),
        scratch_shapes=scratch,
        compiler_params=pltpu.CompilerParams(dimension_semantics=("parallel", "parallel"),
                                             vmem_limit_bytes=VMEM_LIMIT),
        name="gdn",
    )(qkv, qkv, qkv, z, P["conv_w"], P["conv_w"], P["conv_w"], gates, P["gdn_nw"])


def _attn_kernel(q_ref, k_ref, vt_ref, o_ref, m_ref, acc_ref, sta_ref, stb_ref, *, seq, tk):
    tq = q_ref.shape[0]
    q = jnp.concatenate([q_ref[:, g * HD:(g + 1) * HD] for g in range(ATT_GROUP)], axis=0)
    m_ref[...] = jnp.full_like(m_ref, -jnp.inf)
    acc_ref[...] = jnp.zeros_like(acc_ref)
    ones = jnp.ones((ATT_ONES_ROWS, tk), BF16)
    n_kv = seq // tk

    def scores(j, st_ref):
        t0 = pl.multiple_of(jnp.minimum(j, n_kv - 1) * tk, tk)
        st_ref[...] = _dot_nt(k_ref[pl.ds(t0, tk), :], q)

    def consume(j, st_ref):
        t0 = pl.multiple_of(j * tk, tk)
        m_old = m_ref[...]
        m_new = jnp.maximum(m_old, jnp.max(st_ref[...], axis=0, keepdims=True))
        pt = jnp.exp2(st_ref[...] - m_new).astype(BF16)
        alpha = jnp.exp2(m_old - m_new)
        vt = jnp.concatenate([vt_ref[:, pl.ds(t0, tk)], ones], axis=0)
        acc_ref[...] = alpha * acc_ref[...] + _dot(vt, pt)
        m_ref[...] = m_new

    scores(0, sta_ref)

    def body(i, carry):
        scores(2 * i + 1, stb_ref)
        consume(2 * i, sta_ref)
        scores(2 * i + 2, sta_ref)
        consume(2 * i + 1, stb_ref)
        return carry

    lax.fori_loop(0, n_kv // 2, body, 0)
    acc = acc_ref[...]
    out = (acc[:HD] / acc[HD:HD + 1]).T
    for g in range(ATT_GROUP):
        o_ref[:, g * HD:(g + 1) * HD] = out[g * tq:(g + 1) * tq].astype(BF16)


def _attn(q, k, vt, B, T, tq, tk):
    gw = ATT_GROUP * HD
    return pl.pallas_call(
        functools.partial(_attn_kernel, seq=T, tk=tk),
        grid=(B, ATT_KV_HEADS, T // tq),
        in_specs=[
            pl.BlockSpec((None, tq, gw), lambda b, h, i: (b, i, h)),
            pl.BlockSpec((None, T, HD), lambda b, h, i: (b, 0, h)),
            pl.BlockSpec((HD, T), lambda b, h, i: (h, b)),
        ],
        out_specs=pl.BlockSpec((None, tq, gw), lambda b, h, i: (b, i, h)),
        out_shape=jax.ShapeDtypeStruct((B, T, ATT_HEADS * HD), BF16),
        scratch_shapes=[pltpu.VMEM((1, ATT_GROUP * tq), F32),
                        pltpu.VMEM((HD + ATT_ONES_ROWS, ATT_GROUP * tq), F32),
                        pltpu.VMEM((tk, ATT_GROUP * tq), F32), pltpu.VMEM((tk, ATT_GROUP * tq), F32)],
        compiler_params=pltpu.CompilerParams(dimension_semantics=("parallel", "parallel", "parallel"),
                                             vmem_limit_bytes=VMEM_LIMIT),
        name="attn",
    )(q, k, vt)


def _out_mlp_kernel(x_ref, oa_ref, ob_ref, woa_ref, wob_ref, npost_ref, npre2_ref, wup_ref, wdn_ref,
                    npost2_ref, y_ref, *, ff_chunk):
    mix = _dot(oa_ref[...], woa_ref[...]) + _dot(ob_ref[...], wob_ref[...])
    x1 = x_ref[...] + _rms(mix, npost_ref[...])
    hm = _rms(x1, npre2_ref[...]).astype(BF16)
    d_ff = wup_ref.shape[1]
    f = jnp.zeros(x1.shape, F32)
    for cc in range(d_ff // ff_chunk):
        hc = jnp.maximum(_dot(hm, wup_ref[:, cc * ff_chunk:(cc + 1) * ff_chunk]), 0.0)
        f = f + _dot((hc * hc).astype(BF16), wdn_ref[cc * ff_chunk:(cc + 1) * ff_chunk, :])
    y_ref[...] = x1 + _rms(f, npost2_ref[...])


def _out_mlp(x2, oa, ob, P, tm, ff_chunk):
    n_tok, d = x2.shape
    const = lambda i: (0, 0)
    tile = lambda i: (i, 0)
    single = pl.Buffered(1)

    def full(a):
        return pl.BlockSpec(a.shape, const, pipeline_mode=single)

    return pl.pallas_call(
        functools.partial(_out_mlp_kernel, ff_chunk=ff_chunk),
        grid=(n_tok // tm,),
        in_specs=[pl.BlockSpec((tm, d), tile), pl.BlockSpec((tm, oa.shape[1]), tile),
                  pl.BlockSpec((tm, ob.shape[1]), tile),
                  full(P["w_out_a"]), full(P["w_out_b"]), full(P["nw_post"]), full(P["nw_pre2"]),
                  full(P["w_up"]), full(P["w_down"]), full(P["nw_post2"])],
        out_specs=pl.BlockSpec((tm, d), tile),
        out_shape=jax.ShapeDtypeStruct((n_tok, d), F32),
        compiler_params=pltpu.CompilerParams(dimension_semantics=("parallel",), vmem_limit_bytes=VMEM_LIMIT),
        name="out_mlp",
    )(x2, oa, ob, P["w_out_a"], P["w_out_b"], P["nw_post"], P["nw_pre2"], P["w_up"], P["w_down"],
      P["nw_post2"])


def _rope_tables(T):
    n_freq = HD // 4
    t = np.arange(T)
    inv_freq = ROPE_THETA ** (-np.arange(n_freq, dtype=np.float32) / n_freq)
    row_ang = jnp.asarray((t // GRID_W).astype(np.float32))[:, None] * jnp.asarray(inv_freq)[None, :]
    col_ang = jnp.asarray((t % GRID_W).astype(np.float32))[:, None] * jnp.asarray(inv_freq)[None, :]
    cr, sr, cc, sc = jnp.cos(row_ang), jnp.sin(row_ang), jnp.cos(col_ang), jnp.sin(col_ang)
    return jnp.concatenate([cr, cc, cr, cc], axis=1), jnp.concatenate([-sr, -sc, sr, sc], axis=1)


def _chunk_cumsum_mats(tm):
    t = np.arange(tm)
    same = (t[:, None] // CHUNK) == (t[None, :] // CHUNK)
    fwd = same & (t[:, None] <= t[None, :])
    bwd = same & (t[:, None] >= t[None, :])
    return jnp.asarray(fwd, BF16), jnp.asarray(bwd, BF16)


def _prepare(norm_mix_pre, w_in, conv_w, A_log_f, A_log_b, dt_bias_f, dt_bias_b, gdn_norm_w, q_norm_w,
             k_norm_w, w_out, norm_mix_post, norm_mlp_pre, w_up, w_down, norm_mlp_post, seqs, tm):
    gw = GDN_HEADS * HD
    sizes = (3 * gw, gw, GDN_HEADS, GDN_HEADS, GDN_HEADS, GDN_HEADS, ATT_HEADS * HD, ATT_KV_HEADS * HD,
             ATT_KV_HEADS * HD)
    offs = np.concatenate([[0], np.cumsum(sizes)])
    cols = [w_in[:, offs[i]:offs[i + 1]] for i in range(len(sizes))]
    w_qkv, w_z, w_bf, w_bb, w_af, w_ab, w_q, w_k, w_v = cols
    d = w_in.shape[0]
    perm = np.concatenate([np.arange(0, 32), np.arange(64, 96), np.arange(32, 64), np.arange(96, 128)])

    def permute_heads(w, n):
        return w.reshape(d, n, HD)[:, :, perm].reshape(d, n * HD)

    zeros = jnp.zeros((d, GDN_HEADS), F32)
    w_g = jnp.stack([w_bf, w_bb, w_af, w_ab, zeros, zeros, zeros, zeros], axis=2)
    w_g = w_g.reshape(d, GDN_HEADS * GATE_ROWS).T
    zh = jnp.zeros((GDN_HEADS,), F32)
    alog_rows = jnp.stack([zh, zh, A_log_f, A_log_b, zh, zh, zh, zh], axis=1).reshape(-1, 1)
    dtb_rows = jnp.stack([zh, zh, dt_bias_f, dt_bias_b, zh, zh, zh, zh], axis=1).reshape(-1, 1)
    cum_f, cum_b = _chunk_cumsum_mats(IN_SUB)
    cw = jnp.pad(conv_w, ((0, 8 - CONV_K), (0, 0)))
    cw = cw.reshape(8, 3 * GDN_HEADS, HD).transpose(1, 0, 2)
    row = lambda a: a.reshape(1, -1).astype(F32)
    return {
        "nw_pre": row(norm_mix_pre),
        "w_qkv": w_qkv.astype(BF16), "w_z": w_z.astype(BF16), "w_g": w_g.astype(BF16),
        "w_q": permute_heads(w_q, ATT_HEADS).astype(BF16),
        "w_k": permute_heads(w_k, ATT_KV_HEADS).astype(BF16),
        "w_v": w_v.T.astype(BF16),
        "alog_rows": alog_rows, "dtb_rows": dtb_rows, "cum_f": cum_f, "cum_b": cum_b,
        "rope": {T: _rope_tables(T) for T in seqs},
        "qnw": row(q_norm_w[perm]), "knw": row(k_norm_w[perm]),
        "conv_w": cw, "gdn_nw": row(gdn_norm_w),
        "w_out_a": w_out[:gw].astype(BF16), "w_out_b": w_out[gw:].astype(BF16),
        "nw_post": row(norm_mix_post), "nw_pre2": row(norm_mlp_pre),
        "w_up": w_up.astype(BF16), "w_down": w_down.astype(BF16), "nw_post2": row(norm_mlp_post),
    }


def _layer(x, P, tm, tq, tk, ff_chunk):
    B, T, d = x.shape
    x2 = x.reshape(B * T, d)
    qkv, z, gates, q, k, vt = _in_proj(x2, T, P, tm)
    o_a = _gdn(qkv.reshape(B, T, -1), z.reshape(B, T, -1), gates, P, B, T)
    o_b = _attn(q.reshape(B, T, -1), k.reshape(B, T, -1), vt, B, T, tq, min(tk, T))
    y = _out_mlp(x2, o_a.reshape(B * T, -1), o_b.reshape(B * T, -1), P, tm, ff_chunk)
    return y.reshape(B, T, d)


TOKEN_TILE = 512
ATT_Q_TILE = 512
ATT_K_TILE = 512
FF_CHUNK = 1024


def kernel(x_prompt, x_sample, norm_mix_pre, w_in, conv_w, A_log_f, A_log_b, dt_bias_f, dt_bias_b,
           gdn_norm_w, q_norm_w, k_norm_w, w_out, norm_mix_post, norm_mlp_pre, w_up, w_down,
           norm_mlp_post):
    y_prompt, y_sample = x_prompt, x_sample
    for l in range(w_in.shape[0]):
        P = _prepare(norm_mix_pre[l], w_in[l], conv_w[l], A_log_f[l], A_log_b[l], dt_bias_f[l],
                     dt_bias_b[l], gdn_norm_w[l], q_norm_w[l], k_norm_w[l], w_out[l], norm_mix_post[l],
                     norm_mlp_pre[l], w_up[l], w_down[l], norm_mlp_post[l],
                     seqs={y_prompt.shape[1], y_sample.shape[1]}, tm=TOKEN_TILE)
        y_prompt = _layer(y_prompt, P, TOKEN_TILE, ATT_Q_TILE, ATT_K_TILE, FF_CHUNK)
        y_sample = _layer(y_sample, P, TOKEN_TILE, ATT_Q_TILE, ATT_K_TILE, FF_CHUNK)
    return (y_prompt, y_sample)
```

```python
import functools
import math

import jax
import jax.numpy as jnp
import numpy as np
from jax import lax
from jax.experimental import pallas as pl
from jax.experimental.pallas import tpu as pltpu

F32 = jnp.float32
BF16 = jnp.bfloat16

EPS = 1e-6
GRID_W = 64
ROPE_THETA = 10000.0
CONV_K = 5
HD = 128
GDN_HEADS = 4
ATT_HEADS = 4
ATT_KV_HEADS = 2
ATT_GROUP = ATT_HEADS // ATT_KV_HEADS
LANES = 128
CHUNK = LANES
GATE_ROWS = 8
VMEM_LIMIT = 56 * 1024 * 1024
IN_SUB = 256
ATT_ONES_ROWS = 16

_NT = (((1,), (1,)), ((), ()))


def _dot(a, b):
    return jnp.dot(a, b, preferred_element_type=F32)


def _dot_nt(a, b):
    return lax.dot_general(a, b, _NT, preferred_element_type=F32)


def _rms(x, w):
    return x * lax.rsqrt(jnp.mean(x * x, axis=-1, keepdims=True) + EPS) * w


def _sigmoid(x):
    return 1.0 / (1.0 + jnp.exp(-x))


def _in_proj_kernel(x_ref, nw_ref, wqkv_ref, wz_ref, wg_ref, wq_ref, wk_ref, wv_ref,
                    alog_ref, dtb_ref, cumf_ref, cumb_ref, cos_ref, sin_ref, qnw_ref, knw_ref,
                    qkv_ref, z_ref, gate_ref, q_ref, k_ref, vt_ref, *, q_scale):
    for s in range(x_ref.shape[0] // IN_SUB):
        r = slice(s * IN_SUB, (s + 1) * IN_SUB)
        h = _rms(x_ref[r, :], nw_ref[...]).astype(BF16)

        qkv_ref[r, :] = _dot(h, wqkv_ref[...]).astype(BF16)
        z_ref[r, :] = _dot(h, wz_ref[...]).astype(BF16)
        vt_ref[:, r] = _dot_nt(wv_ref[...], h).astype(BF16)

        graw = _dot_nt(wg_ref[...], h)
        kind = lax.broadcasted_iota(jnp.int32, graw.shape, 0) % GATE_ROWS
        beta = _sigmoid(graw)
        a = graw + dtb_ref[...]
        softplus = jnp.maximum(a, 0.0) + jnp.log1p(jnp.exp(-jnp.abs(a)))
        g = -jnp.exp(alog_ref[...]) * softplus
        g = jnp.where(kind >= 2, g, 0.0)
        p1 = g.astype(BF16)
        r1 = g - p1.astype(F32)
        p2 = r1.astype(BF16)
        p3 = (r1 - p2.astype(F32)).astype(BF16)
        nrow = g.shape[0]
        parts = jnp.concatenate([p1, p2, p3], axis=0)
        cf = _dot(parts, cumf_ref[...])
        cb = _dot(parts, cumb_ref[...])
        cf = cf[:nrow] + cf[nrow:2 * nrow] + cf[2 * nrow:]
        cb = cb[:nrow] + cb[nrow:2 * nrow] + cb[2 * nrow:]
        gate_ref[:, r] = jnp.where(kind < 2, beta, jnp.where(kind == 2, cf, jnp.where(kind == 3, cb, 0.0)))

        cos = cos_ref[r, :]
        sin = sin_ref[r, :]

        def norm_rope(y, w, scale):
            y = _rms(y, w)
            y = y * cos + pltpu.roll(y, HD // 2, axis=1) * sin
            return (y * scale).astype(BF16) if scale != 1.0 else y.astype(BF16)

        qa = _dot(h, wq_ref[...])
        for hh in range(ATT_HEADS):
            q_ref[r, hh * HD:(hh + 1) * HD] = norm_rope(qa[:, hh * HD:(hh + 1) * HD], qnw_ref[...], q_scale)
        ka = _dot(h, wk_ref[...])
        for hh in range(ATT_KV_HEADS):
            k_ref[r, hh * HD:(hh + 1) * HD] = norm_rope(ka[:, hh * HD:(hh + 1) * HD], knw_ref[...], 1.0)


def _in_proj(x2, T, P, tm):
    n_tok, d = x2.shape
    nt = n_tok // tm
    tpb = T // tm
    const = lambda i: (0, 0)
    tile = lambda i: (i, 0)

    def full(a):
        return pl.BlockSpec(a.shape, const)

    in_specs = [
        pl.BlockSpec((tm, d), tile), full(P["nw_pre"]),
        full(P["w_qkv"]), full(P["w_z"]), full(P["w_g"]), full(P["w_q"]), full(P["w_k"]), full(P["w_v"]),
        full(P["alog_rows"]), full(P["dtb_rows"]), full(P["cum_f"]), full(P["cum_b"]),
        pl.BlockSpec((tm, HD), lambda i: (i % tpb, 0)), pl.BlockSpec((tm, HD), lambda i: (i % tpb, 0)),
        full(P["qnw"]), full(P["knw"]),
    ]
    n_gate = GDN_HEADS * GATE_ROWS
    out_shape = [
        jax.ShapeDtypeStruct((n_tok, 3 * GDN_HEADS * HD), BF16),
        jax.ShapeDtypeStruct((n_tok, GDN_HEADS * HD), BF16),
        jax.ShapeDtypeStruct((n_gate, n_tok), F32),
        jax.ShapeDtypeStruct((n_tok, ATT_HEADS * HD), BF16),
        jax.ShapeDtypeStruct((n_tok, ATT_KV_HEADS * HD), BF16),
        jax.ShapeDtypeStruct((ATT_KV_HEADS * HD, n_tok), BF16),
    ]
    out_specs = [
        pl.BlockSpec((tm, 3 * GDN_HEADS * HD), tile),
        pl.BlockSpec((tm, GDN_HEADS * HD), tile),
        pl.BlockSpec((n_gate, tm), lambda i: (0, i)),
        pl.BlockSpec((tm, ATT_HEADS * HD), tile),
        pl.BlockSpec((tm, ATT_KV_HEADS * HD), tile),
        pl.BlockSpec((ATT_KV_HEADS * HD, tm), lambda i: (0, i)),
    ]
    q_scale = (HD ** -0.5) * math.log2(math.e)
    cos, sin = P["rope"][T]
    return pl.pallas_call(
        functools.partial(_in_proj_kernel, q_scale=q_scale),
        grid=(nt,), in_specs=in_specs, out_specs=out_specs, out_shape=out_shape,
        compiler_params=pltpu.CompilerParams(dimension_semantics=("parallel",), vmem_limit_bytes=VMEM_LIMIT),
        name="in_proj",
    )(x2, P["nw_pre"], P["w_qkv"], P["w_z"], P["w_g"], P["w_q"], P["w_k"], P["w_v"],
      P["alog_rows"], P["dtb_rows"], P["cum_f"], P["cum_b"], cos, sin, P["qnw"], P["knw"])


PREP_CHUNKS = 4
GROUP_ROWS = PREP_CHUNKS * CHUNK
CONV_ROWS = GROUP_ROWS // 2
CONV_HALO = 16


def _conv_silu(x_ref, cw, win_ref, t0, seq):
    r, hl = CONV_ROWS, CONV_HALO
    prev = x_ref[pl.ds(pl.multiple_of(jnp.maximum(t0 - hl, 0), hl), hl), :].astype(F32)
    nxt = x_ref[pl.ds(pl.multiple_of(jnp.minimum(t0 + r, seq - hl), hl), hl), :].astype(F32)
    win_ref[0:hl, :] = jnp.where(t0 > 0, prev, 0.0)
    win_ref[hl:hl + r, :] = x_ref[pl.ds(t0, r), :].astype(F32)
    win_ref[hl + r:hl + r + hl, :] = jnp.where(t0 + r < seq, nxt, 0.0)
    acc = jnp.zeros((r, LANES), F32)
    for j in range(CONV_K):
        off = hl - CONV_K // 2 + j
        acc = acc + cw[j:j + 1, :] * win_ref[off:off + r, :]
    return acc * _sigmoid(acc)


def _l2n(x):
    return x * lax.rsqrt(jnp.sum(x * x, axis=-1, keepdims=True) + EPS)


def _gdn_kernel(q_ref, k_ref, v_ref, z_ref, cwq_ref, cwk_ref, cwv_ref, gate_ref, nw_ref, o_ref,
                qn_ref, kn_ref, vn_ref, ob_ref, winq_ref, wink_ref, winv_ref,
                mqf_ref, nf_ref, decf_ref, mqb_ref, nb_ref, decb_ref, *, seq):
    n_chunk = seq // CHUNK
    c = CHUNK
    dir_refs = ((mqf_ref, nf_ref, decf_ref), (mqb_ref, nb_ref, decb_ref))

    cwq, cwk, cwv = cwq_ref[0], cwk_ref[0], cwv_ref[0]
    n_group = seq // GROUP_ROWS

    def conv_pieces(g):
        def piece(x_ref, cw, win_ref, dst_ref, normalise, scale, half):
            def run():
                t0 = pl.multiple_of(g * GROUP_ROWS + half * CONV_ROWS, CONV_ROWS)
                y = _conv_silu(x_ref, cw, win_ref, t0, seq)
                if normalise:
                    y = _l2n(y) * scale if scale != 1.0 else _l2n(y)
                dst_ref[pl.ds(t0, CONV_ROWS), :] = y.astype(BF16)
            return run
        streams = ((q_ref, cwq, winq_ref, qn_ref, True, HD ** -0.5), (k_ref, cwk, wink_ref, kn_ref, True, 1.0),
                   (v_ref, cwv, winv_ref, vn_ref, False, 1.0))
        return [piece(*s, half) for s in streams for half in range(GROUP_ROWS // CONV_ROWS)]

    ri = lax.broadcasted_iota(jnp.int32, (c, c), 0)
    ci = lax.broadcasted_iota(jnp.int32, (c, c), 1)
    eye = ri == ci
    eye_bf = jnp.where(eye, 1.0, 0.0).astype(BF16)

    def col(row):
        return jnp.sum(jnp.where(eye, row, 0.0), axis=1, keepdims=True)

    def inv_unit_tri(mats, fillers):
        ts = [jnp.where(eye, 1.0, 0.0) - jnp.where((ri // 2) == (ci // 2), a, 0.0) for a in mats]
        fillers = list(fillers)
        per_level = -(-len(fillers) // (c.bit_length() - 2))
        b = 2
        while b < c:
            off = ((ri // (2 * b)) == (ci // (2 * b))) & ((ri // b) != (ci // b))
            tbs = [t.astype(BF16) for t in ts]
            xs = [_dot(jnp.where(off, a, 0.0).astype(BF16), tb).astype(BF16) for a, tb in zip(mats, tbs)]
            for f in fillers[:per_level]:
                f()
            fillers = fillers[per_level:]
            ts = [t - _dot(tb, x) for t, tb, x in zip(ts, tbs, xs)]
            b *= 2
        return ts

    def prep_group(probs, first, fillers):
        t0s = [pl.multiple_of(ch * c, c) for ch, d in probs]
        qs = [qn_ref[pl.ds(t0, c), :] for t0 in t0s]
        ks = [kn_ref[pl.ds(t0, c), :] for t0 in t0s]
        vs = [vn_ref[pl.ds(t0, c), :] for t0 in t0s]
        gts = [gate_ref[:, pl.ds(t0, c)] for t0 in t0s]
        kqk = [_dot_nt(jnp.concatenate([k, q, eye_bf], axis=0), k) for q, k in zip(qs, ks)]
        grams = [x[:c] for x in kqk]
        qks = [x[c:2 * c] for x in kqk]
        kts = [x[2 * c:] for x in kqk]
        beta_r = [g[d:d + 1] for g, (ch, d) in zip(gts, probs)]
        gc_r = [g[2 + d:3 + d] for g, (ch, d) in zip(gts, probs)]
        gc_c = [col(g) for g in gc_r]
        beta_c = [col(bb) for bb in beta_r]
        masks = [(ri <= ci) if d else (ri >= ci) for ch, d in probs]
        stricts = [(ri < ci) if d else (ri > ci) for ch, d in probs]
        es = [jnp.where(m, jnp.exp(jnp.where(m, gcc - gcr, 0.0)), 0.0) for m, gcc, gcr in zip(masks, gc_c, gc_r)]
        amats = [jnp.where(st, gram * e * bc, 0.0) for gram, st, e, bc in zip(grams, stricts, es, beta_c)]
        ts = inv_unit_tri(amats, fillers)
        us = [_dot((t * br).astype(BF16), v) for t, br, v in zip(ts, beta_r, vs)]
        ws = [_dot((t * (br * jnp.exp(gr))).astype(BF16), k) for t, br, gr, k in zip(ts, beta_r, gc_r, ks)]
        wus = [jnp.concatenate([w, u], axis=1).astype(BF16) for w, u in zip(ws, us)]
        g_last = [gr[:, 0:1] if d else gr[:, c - 1:c] for (ch, d), gr in zip(probs, gc_r)]
        kes = [(kt * jnp.exp(gl - gr)).astype(BF16) for kt, gl, gr in zip(kts, g_last, gc_r)]
        qkms = [(qk * e).astype(BF16) for qk, e in zip(qks, es)]
        mq = [_dot(jnp.concatenate([ke, qkm], axis=0), wu) for ke, qkm, wu in zip(kes, qkms, wus)]
        for i, (ch, d) in enumerate(probs):
            mn, qo = mq[i][:c], mq[i][c:]
            mq_ref, n_ref, dec_ref = dir_refs[d]
            m0 = pl.multiple_of(ch * (2 * c), 2 * c)
            mq_ref[pl.ds(m0, c), :] = mn[:, :HD].astype(BF16)
            q_dec = qs[i].astype(F32) * jnp.exp(gc_c[i])
            mq_ref[pl.ds(m0 + c, c), :] = (q_dec - qo[:, :HD]).astype(BF16)
            n_ref[pl.ds(t0s[i], c), :] = mn[:, HD:]
            dec_ref[pl.ds(pl.multiple_of(ch * 8, 8), 8), :] = jnp.broadcast_to(jnp.exp(g_last[i]), (8, HD))
            if first:
                ob_ref[pl.ds(t0s[i], c), :] = qo[:, HD:]
            else:
                ob_ref[pl.ds(t0s[i], c), :] += qo[:, HD:]

    def step(chunk, s, refs):
        mq_ref, n_ref, dec_ref = refs
        t0 = pl.multiple_of(chunk * c, c)
        m0 = pl.multiple_of(chunk * (2 * c), 2 * c)
        s_bf = s.astype(BF16)
        ms = _dot(mq_ref[pl.ds(m0, c), :], s_bf)
        dec = dec_ref[pl.ds(pl.multiple_of(chunk * 8, 8), 8), :][0:1]
        ob_ref[pl.ds(t0, c), :] += _dot(mq_ref[pl.ds(m0 + c, c), :], s_bf)
        return s * dec - ms + n_ref[pl.ds(t0, c), :]

    half = n_group // 2

    def group_chunks(g, reverse=False):
        chunks = [g * PREP_CHUNKS + cc for cc in range(PREP_CHUNKS)]
        return chunks[::-1] if reverse else chunks

    def scan_steps(g, state):
        def one(d, chunk):
            def run():
                state[d] = step(chunk, state[d], dir_refs[d])
            return run
        pairs = zip(group_chunks(g), group_chunks(n_group - 1 - g, reverse=True))
        return [f for cf, cb in pairs for f in (one(0, cf), one(1, cb))]

    def next_convs(g):
        return conv_pieces(jnp.minimum(g + 1, half - 1)) + conv_pieces(jnp.maximum(n_group - 2 - g, half))

    def iteration(g, first, fillers):
        probs = [(ch, 0) for ch in group_chunks(g)] + [(ch, 1) for ch in group_chunks(n_group - 1 - g)]
        prep_group(probs, first, fillers)

    for run in conv_pieces(0) + conv_pieces(n_group - 1):
        run()
    iteration(0, True, next_convs(0))

    def first_half(g, carry):
        state = list(carry)
        iteration(g, True, scan_steps(g - 1, state) + next_convs(g))
        return tuple(state)

    def second_half(g, carry):
        state = list(carry)
        iteration(g, False, scan_steps(g - 1, state))
        return tuple(state)

    s0 = jnp.zeros((HD, HD), F32)
    carry = lax.fori_loop(1, half, first_half, (s0, s0))
    carry = lax.fori_loop(half, n_group, second_half, carry)
    state = list(carry)
    for run in scan_steps(n_group - 1, state):
        run()

    nw = nw_ref[...]

    def norm_body(i, carry):
        t0 = pl.multiple_of(i * CONV_ROWS, CONV_ROWS)
        o = _rms(ob_ref[pl.ds(t0, CONV_ROWS), :], nw)
        zz = z_ref[pl.ds(t0, CONV_ROWS), :].astype(F32)
        o_ref[pl.ds(t0, CONV_ROWS), :] = (o * (zz * _sigmoid(zz))).astype(BF16)
        return carry

    lax.fori_loop(0, seq // CONV_ROWS, norm_body, 0)


def _gdn(qkv, z, gates, P, B, T):
    nh = GDN_HEADS
    single = pl.Buffered(1)
    in_specs = [
        pl.BlockSpec((None, T, HD), lambda b, h: (b, 0, h), pipeline_mode=single),
        pl.BlockSpec((None, T, HD), lambda b, h: (b, 0, nh + h), pipeline_mode=single),
        pl.BlockSpec((None, T, HD), lambda b, h: (b, 0, 2 * nh + h), pipeline_mode=single),
        pl.BlockSpec((None, T, HD), lambda b, h: (b, 0, h), pipeline_mode=single),
        pl.BlockSpec((1, 8, HD), lambda b, h: (h, 0, 0)),
        pl.BlockSpec((1, 8, HD), lambda b, h: (nh + h, 0, 0)),
        pl.BlockSpec((1, 8, HD), lambda b, h: (2 * nh + h, 0, 0)),
        pl.BlockSpec((GATE_ROWS, T), lambda b, h: (h, b), pipeline_mode=single),
        pl.BlockSpec((1, HD), lambda b, h: (0, 0)),
    ]
    scratch = [
        pltpu.VMEM((T, HD), BF16), pltpu.VMEM((T, HD), BF16), pltpu.VMEM((T, HD), BF16),
        pltpu.VMEM((T, HD), F32),
        pltpu.VMEM((CONV_ROWS + 2 * CONV_HALO, HD), F32),
        pltpu.VMEM((CONV_ROWS + 2 * CONV_HALO, HD), F32),
        pltpu.VMEM((CONV_ROWS + 2 * CONV_HALO, HD), F32),
    ]
    per_direction = [pltpu.VMEM((2 * T, HD), BF16), pltpu.VMEM((T, HD), F32),
                     pltpu.VMEM((T // CHUNK * 8, HD), F32)]
    scratch = scratch + per_direction + per_direction
    return pl.pallas_call(
        functools.partial(_gdn_kernel, seq=T),
        grid=(B, nh), in_specs=in_specs,
        out_specs=pl.BlockSpec((None, T, HD), lambda b, h: (b, 0, h)),
        out_shape=jax.ShapeDtypeStruct((B, T, nh * HD), BF16),
        scratch_shapes=scratch,
        compiler_params=pltpu.CompilerParams(dimension_semantics=("parallel", "parallel"),
                                             vmem_limit_bytes=VMEM_LIMIT),
        name="gdn",
    )(qkv, qkv, qkv, z, P["conv_w"], P["conv_w"], P["conv_w"], gates, P["gdn_nw"])


def _attn_kernel(q_ref, k_ref, vt_ref, o_ref, m_ref, acc_ref, sta_ref, stb_ref, *, seq, tk):
    tq = q_ref.shape[0]
    q = jnp.concatenate([q_ref[:, g * HD:(g + 1) * HD] for g in range(ATT_GROUP)], axis=0)
    m_ref[...] = jnp.full_like(m_ref, -jnp.inf)
    acc_ref[...] = jnp.zeros_like(acc_ref)
    ones = jnp.ones((ATT_ONES_ROWS, tk), BF16)
    n_kv = seq // tk

    def scores(j, st_ref):
        t0 = pl.multiple_of(jnp.minimum(j, n_kv - 1) * tk, tk)
        st_ref[...] = _dot_nt(k_ref[pl.ds(t0, tk), :], q)

    def consume(j, st_ref):
        t0 = pl.multiple_of(j * tk, tk)
        m_old = m_ref[...]
        m_new = jnp.maximum(m_old, jnp.max(st_ref[...], axis=0, keepdims=True))
        pt = jnp.exp2(st_ref[...] - m_new).astype(BF16)
        alpha = jnp.exp2(m_old - m_new)
        vt = jnp.concatenate([vt_ref[:, pl.ds(t0, tk)], ones], axis=0)
        acc_ref[...] = alpha * acc_ref[...] + _dot(vt, pt)
        m_ref[...] = m_new

    scores(0, sta_ref)

    def body(i, carry):
        scores(2 * i + 1, stb_ref)
        consume(2 * i, sta_ref)
        scores(2 * i + 2, sta_ref)
        consume(2 * i + 1, stb_ref)
        return carry

    lax.fori_loop(0, n_kv // 2, body, 0)
    acc = acc_ref[...]
    out = (acc[:HD] / acc[HD:HD + 1]).T
    for g in range(ATT_GROUP):
        o_ref[:, g * HD:(g + 1) * HD] = out[g * tq:(g + 1) * tq].astype(BF16)


def _attn(q, k, vt, B, T, tq, tk):
    gw = ATT_GROUP * HD
    return pl.pallas_call(
        functools.partial(_attn_kernel, seq=T, tk=tk),
        grid=(B, ATT_KV_HEADS, T // tq),
        in_specs=[
            pl.BlockSpec((None, tq, gw), lambda b, h, i: (b, i, h)),
            pl.BlockSpec((None, T, HD), lambda b, h, i: (b, 0, h)),
            pl.BlockSpec((HD, T), lambda b, h, i: (h, b)),
        ],
        out_specs=pl.BlockSpec((None, tq, gw), lambda b, h, i: (b, i, h)),
        out_shape=jax.ShapeDtypeStruct((B, T, ATT_HEADS * HD), BF16),
        scratch_shapes=[pltpu.VMEM((1, ATT_GROUP * tq), F32),
                        pltpu.VMEM((HD + ATT_ONES_ROWS, ATT_GROUP * tq), F32),
                        pltpu.VMEM((tk, ATT_GROUP * tq), F32), pltpu.VMEM((tk, ATT_GROUP * tq), F32)],
        compiler_params=pltpu.CompilerParams(dimension_semantics=("parallel", "parallel", "parallel"),
                                             vmem_limit_bytes=VMEM_LIMIT),
        name="attn",
    )(q, k, vt)


def _out_mlp_kernel(x_ref, oa_ref, ob_ref, woa_ref, wob_ref, npost_ref, npre2_ref, wup_ref, wdn_ref,
                    npost2_ref, y_ref, *, ff_chunk):
    mix = _dot(oa_ref[...], woa_ref[...]) + _dot(ob_ref[...], wob_ref[...])
    x1 = x_ref[...] + _rms(mix, npost_ref[...])
    hm = _rms(x1, npre2_ref[...]).astype(BF16)
    d_ff = wup_ref.shape[1]
    f = jnp.zeros(x1.shape, F32)
    for cc in range(d_ff // ff_chunk):
        hc = jnp.maximum(_dot(hm, wup_ref[:, cc * ff_chunk:(cc + 1) * ff_chunk]), 0.0)
        f = f + _dot((hc * hc).astype(BF16), wdn_ref[cc * ff_chunk:(cc + 1) * ff_chunk, :])
    y_ref[...] = x1 + _rms(f, npost2_ref[...])


def _out_mlp(x2, oa, ob, P, tm, ff_chunk):
    n_tok, d = x2.shape
    const = lambda i: (0, 0)
    tile = lambda i: (i, 0)
    single = pl.Buffered(1)

    def full(a):
        return pl.BlockSpec(a.shape, const, pipeline_mode=single)

    return pl.pallas_call(
        functools.partial(_out_mlp_kernel, ff_chunk=ff_chunk),
        grid=(n_tok // tm,),
        in_specs=[pl.BlockSpec((tm, d), tile), pl.BlockSpec((tm, oa.shape[1]), tile),
                  pl.BlockSpec((tm, ob.shape[1]), tile),
                  full(P["w_out_a"]), full(P["w_out_b"]), full(P["nw_post"]), full(P["nw_pre2"]),
                  full(P["w_up"]), full(P["w_down"]), full(P["nw_post2"])],
        out_specs=pl.BlockSpec((tm, d), tile),
        out_shape=jax.ShapeDtypeStruct((n_tok, d), F32),
        compiler_params=pltpu.CompilerParams(dimension_semantics=("parallel",), vmem_limit_bytes=VMEM_LIMIT),
        name="out_mlp",
    )(x2, oa, ob, P["w_out_a"], P["w_out_b"], P["nw_post"], P["nw_pre2"], P["w_up"], P["w_down"],
      P["nw_post2"])


def _rope_tables(T):
    n_freq = HD // 4
    t = np.arange(T)
    inv_freq = ROPE_THETA ** (-np.arange(n_freq, dtype=np.float32) / n_freq)
    row_ang = jnp.asarray((t // GRID_W).astype(np.float32))[:, None] * jnp.asarray(inv_freq)[None, :]
    col_ang = jnp.asarray((t % GRID_W).astype(np.float32))[:, None] * jnp.asarray(inv_freq)[None, :]
    cr, sr, cc, sc = jnp.cos(row_ang), jnp.sin(row_ang), jnp.cos(col_ang), jnp.sin(col_ang)
    return jnp.concatenate([cr, cc, cr, cc], axis=1), jnp.concatenate([-sr, -sc, sr, sc], axis=1)


def _chunk_cumsum_mats(tm):
    t = np.arange(tm)
    same = (t[:, None] // CHUNK) == (t[None, :] // CHUNK)
    fwd = same & (t[:, None] <= t[None, :])
    bwd = same & (t[:, None] >= t[None, :])
    return jnp.asarray(fwd, BF16), jnp.asarray(bwd, BF16)


def _prepare(norm_mix_pre, w_in, conv_w, A_log_f, A_log_b, dt_bias_f, dt_bias_b, gdn_norm_w, q_norm_w,
             k_norm_w, w_out, norm_mix_post, norm_mlp_pre, w_up, w_down, norm_mlp_post, seqs):
    gw = GDN_HEADS * HD
    sizes = (3 * gw, gw, GDN_HEADS, GDN_HEADS, GDN_HEADS, GDN_HEADS, ATT_HEADS * HD, ATT_KV_HEADS * HD,
             ATT_KV_HEADS * HD)
    offs = np.concatenate([[0], np.cumsum(sizes)])
    cols = [w_in[:, offs[i]:offs[i + 1]] for i in range(len(sizes))]
    w_qkv, w_z, w_bf, w_bb, w_af, w_ab, w_q, w_k, w_v = cols
    d = w_in.shape[0]
    perm = np.concatenate([np.arange(0, 32), np.arange(64, 96), np.arange(32, 64), np.arange(96, 128)])

    def permute_heads(w, n):
        return w.reshape(d, n, HD)[:, :, perm].reshape(d, n * HD)

    zeros = jnp.zeros((d, GDN_HEADS), F32)
    w_g = jnp.stack([w_bf, w_bb, w_af, w_ab, zeros, zeros, zeros, zeros], axis=2)
    w_g = w_g.reshape(d, GDN_HEADS * GATE_ROWS).T
    zh = jnp.zeros((GDN_HEADS,), F32)
    alog_rows = jnp.stack([zh, zh, A_log_f, A_log_b, zh, zh, zh, zh], axis=1).reshape(-1, 1)
    dtb_rows = jnp.stack([zh, zh, dt_bias_f, dt_bias_b, zh, zh, zh, zh], axis=1).reshape(-1, 1)
    cum_f, cum_b = _chunk_cumsum_mats(IN_SUB)
    cw = jnp.pad(conv_w, ((0, 8 - CONV_K), (0, 0)))
    cw = cw.reshape(8, 3 * GDN_HEADS, HD).transpose(1, 0, 2)
    row = lambda a: a.reshape(1, -1).astype(F32)
    return {
        "nw_pre": row(norm_mix_pre),
        "w_qkv": w_qkv.astype(BF16), "w_z": w_z.astype(BF16), "w_g": w_g.astype(BF16),
        "w_q": permute_heads(w_q, ATT_HEADS).astype(BF16),
        "w_k": permute_heads(w_k, ATT_KV_HEADS).astype(BF16),
        "w_v": w_v.T.astype(BF16),
        "alog_rows": alog_rows, "dtb_rows": dtb_rows, "cum_f": cum_f, "cum_b": cum_b,
        "rope": {T: _rope_tables(T) for T in seqs},
        "qnw": row(q_norm_w[perm]), "knw": row(k_norm_w[perm]),
        "conv_w": cw, "gdn_nw": row(gdn_norm_w),
        "w_out_a": w_out[:gw].astype(BF16), "w_out_b": w_out[gw:].astype(BF16),
        "nw_post": row(norm_mix_post), "nw_pre2": row(norm_mlp_pre),
        "w_up": w_up.astype(BF16), "w_down": w_down.astype(BF16), "nw_post2": row(norm_mlp_post),
    }


IN_TILE = 1024
MLP_TILE = 512
ATT_Q_TILE = 512
ATT_K_TILE = 1024
FF_CHUNK = 1024


def _layer(x, P):
    B, T, d = x.shape
    x2 = x.reshape(B * T, d)
    qkv, z, gates, q, k, vt = _in_proj(x2, T, P, min(IN_TILE, T))
    o_a = _gdn(qkv.reshape(B, T, -1), z.reshape(B, T, -1), gates, P, B, T)
    o_b = _attn(q.reshape(B, T, -1), k.reshape(B, T, -1), vt, B, T, min(ATT_Q_TILE, T), min(ATT_K_TILE, T // 2))
    y = _out_mlp(x2, o_a.reshape(B * T, -1), o_b.reshape(B * T, -1), P, min(MLP_TILE, T), FF_CHUNK)
    return y.reshape(B, T, d)


def kernel(x_prompt, x_sample, norm_mix_pre, w_in, conv_w, A_log_f, A_log_b, dt_bias_f, dt_bias_b,
           gdn_norm_w, q_norm_w, k_norm_w, w_out, norm_mix_post, norm_mlp_pre, w_up, w_down,
           norm_mlp_post):
    y_prompt, y_sample = x_prompt, x_sample
    for l in range(w_in.shape[0]):
        P = _prepare(norm_mix_pre[l], w_in[l], conv_w[l], A_log_f[l], A_log_b[l], dt_bias_f[l],
                     dt_bias_b[l], gdn_norm_w[l], q_norm_w[l], k_norm_w[l], w_out[l], norm_mix_post[l],
                     norm_mlp_pre[l], w_up[l], w_down[l], norm_mlp_post[l],
                     seqs={y_prompt.shape[1], y_sample.shape[1]})
        y_prompt = _layer(y_prompt, P)
        y_sample = _layer(y_sample, P)
    return (y_prompt, y_sample)
```

```python
import functools
import math

import jax
import jax.numpy as jnp
import numpy as np
from jax import lax
from jax.experimental import pallas as pl
from jax.experimental.pallas import tpu as pltpu

F32 = jnp.float32
BF16 = jnp.bfloat16

EPS = 1e-6
GRID_W = 64
ROPE_THETA = 10000.0
CONV_K = 5
HD = 128
GDN_HEADS = 4
ATT_HEADS = 4
ATT_KV_HEADS = 2
ATT_GROUP = ATT_HEADS // ATT_KV_HEADS
LANES = 128
CHUNK = LANES
GATE_ROWS = 8
VMEM_LIMIT = 56 * 1024 * 1024
IN_SUB = 256
ATT_ONES_ROWS = 16

_NT = (((1,), (1,)), ((), ()))


def _dot(a, b):
    return jnp.dot(a, b, preferred_element_type=F32)


def _dot_nt(a, b):
    return lax.dot_general(a, b, _NT, preferred_element_type=F32)


def _rms(x, w):
    return x * lax.rsqrt(jnp.mean(x * x, axis=-1, keepdims=True) + EPS) * w


def _sigmoid(x):
    return 1.0 / (1.0 + jnp.exp(-x))


def _in_proj_kernel(x_ref, nw_ref, wqkv_ref, wz_ref, wg_ref, wq_ref, wk_ref, wv_ref,
                    alog_ref, dtb_ref, cumf_ref, cumb_ref, cos_ref, sin_ref, qnw_ref, knw_ref,
                    qkv_ref, z_ref, gate_ref, q_ref, k_ref, vt_ref, *, q_scale):
    for s in range(x_ref.shape[0] // IN_SUB):
        r = slice(s * IN_SUB, (s + 1) * IN_SUB)
        h = _rms(x_ref[r, :], nw_ref[...]).astype(BF16)

        qkv_ref[r, :] = _dot(h, wqkv_ref[...]).astype(BF16)
        z_ref[r, :] = _dot(h, wz_ref[...]).astype(BF16)
        vt_ref[:, r] = _dot_nt(wv_ref[...], h).astype(BF16)

        graw = _dot_nt(wg_ref[...], h)
        kind = lax.broadcasted_iota(jnp.int32, graw.shape, 0) % GATE_ROWS
        beta = _sigmoid(graw)
        a = graw + dtb_ref[...]
        softplus = jnp.maximum(a, 0.0) + jnp.log1p(jnp.exp(-jnp.abs(a)))
        g = -jnp.exp(alog_ref[...]) * softplus
        g = jnp.where(kind >= 2, g, 0.0)
        p1 = g.astype(BF16)
        r1 = g - p1.astype(F32)
        p2 = r1.astype(BF16)
        p3 = (r1 - p2.astype(F32)).astype(BF16)
        nrow = g.shape[0]
        parts = jnp.concatenate([p1, p2, p3], axis=0)
        cf = _dot(parts, cumf_ref[...])
        cb = _dot(parts, cumb_ref[...])
        cf = cf[:nrow] + cf[nrow:2 * nrow] + cf[2 * nrow:]
        cb = cb[:nrow] + cb[nrow:2 * nrow] + cb[2 * nrow:]
        gate_ref[:, r] = jnp.where(kind < 2, beta, jnp.where(kind == 2, cf, jnp.where(kind == 3, cb, 0.0)))

        cos = cos_ref[r, :]
        sin = sin_ref[r, :]

        def norm_rope(y, w, scale):
            y = _rms(y, w)
            y = y * cos + pltpu.roll(y, HD // 2, axis=1) * sin
            return (y * scale).astype(BF16) if scale != 1.0 else y.astype(BF16)

        qa = _dot(h, wq_ref[...])
        for hh in range(ATT_HEADS):
            q_ref[r, hh * HD:(hh + 1) * HD] = norm_rope(qa[:, hh * HD:(hh + 1) * HD], qnw_ref[...], q_scale)
        ka = _dot(h, wk_ref[...])
        for hh in range(ATT_KV_HEADS):
            k_ref[r, hh * HD:(hh + 1) * HD] = norm_rope(ka[:, hh * HD:(hh + 1) * HD], knw_ref[...], 1.0)


def _in_proj(x2, T, P, tm):
    n_tok, d = x2.shape
    nt = n_tok // tm
    tpb = T // tm
    const = lambda i: (0, 0)
    tile = lambda i: (i, 0)

    def full(a):
        return pl.BlockSpec(a.shape, const)

    in_specs = [
        pl.BlockSpec((tm, d), tile), full(P["nw_pre"]),
        full(P["w_qkv"]), full(P["w_z"]), full(P["w_g"]), full(P["w_q"]), full(P["w_k"]), full(P["w_v"]),
        full(P["alog_rows"]), full(P["dtb_rows"]), full(P["cum_f"]), full(P["cum_b"]),
        pl.BlockSpec((tm, HD), lambda i: (i % tpb, 0)), pl.BlockSpec((tm, HD), lambda i: (i % tpb, 0)),
        full(P["qnw"]), full(P["knw"]),
    ]
    n_gate = GDN_HEADS * GATE_ROWS
    out_shape = [
        jax.ShapeDtypeStruct((n_tok, 3 * GDN_HEADS * HD), BF16),
        jax.ShapeDtypeStruct((n_tok, GDN_HEADS * HD), BF16),
        jax.ShapeDtypeStruct((n_gate, n_tok), F32),
        jax.ShapeDtypeStruct((n_tok, ATT_HEADS * HD), BF16),
        jax.ShapeDtypeStruct((n_tok, ATT_KV_HEADS * HD), BF16),
        jax.ShapeDtypeStruct((ATT_KV_HEADS * HD, n_tok), BF16),
    ]
    out_specs = [
        pl.BlockSpec((tm, 3 * GDN_HEADS * HD), tile),
        pl.BlockSpec((tm, GDN_HEADS * HD), tile),
        pl.BlockSpec((n_gate, tm), lambda i: (0, i)),
        pl.BlockSpec((tm, ATT_HEADS * HD), tile),
        pl.BlockSpec((tm, ATT_KV_HEADS * HD), tile),
        pl.BlockSpec((ATT_KV_HEADS * HD, tm), lambda i: (0, i)),
    ]
    q_scale = (HD ** -0.5) * math.log2(math.e)
    cos, sin = P["rope"][T]
    return pl.pallas_call(
        functools.partial(_in_proj_kernel, q_scale=q_scale),
        grid=(nt,), in_specs=in_specs, out_specs=out_specs, out_shape=out_shape,
        compiler_params=pltpu.CompilerParams(dimension_semantics=("parallel",), vmem_limit_bytes=VMEM_LIMIT),
        name="in_proj",
    )(x2, P["nw_pre"], P["w_qkv"], P["w_z"], P["w_g"], P["w_q"], P["w_k"], P["w_v"],
      P["alog_rows"], P["dtb_rows"], P["cum_f"], P["cum_b"], cos, sin, P["qnw"], P["knw"])


PREP_CHUNKS = 4
GROUP_ROWS = PREP_CHUNKS * CHUNK
CONV_ROWS = GROUP_ROWS // 2
CONV_HALO = 16


def _conv_silu(x_ref, cw, win_ref, t0, seq):
    r, hl = CONV_ROWS, CONV_HALO
    prev = x_ref[pl.ds(pl.multiple_of(jnp.maximum(t0 - hl, 0), hl), hl), :].astype(F32)
    nxt = x_ref[pl.ds(pl.multiple_of(jnp.minimum(t0 + r, seq - hl), hl), hl), :].astype(F32)
    win_ref[0:hl, :] = jnp.where(t0 > 0, prev, 0.0)
    win_ref[hl:hl + r, :] = x_ref[pl.ds(t0, r), :].astype(F32)
    win_ref[hl + r:hl + r + hl, :] = jnp.where(t0 + r < seq, nxt, 0.0)
    acc = jnp.zeros((r, LANES), F32)
    for j in range(CONV_K):
        off = hl - CONV_K // 2 + j
        acc = acc + cw[j:j + 1, :] * win_ref[off:off + r, :]
    return acc * _sigmoid(acc)


def _l2n(x):
    return x * lax.rsqrt(jnp.sum(x * x, axis=-1, keepdims=True) + EPS)


def _gdn_kernel(q_ref, k_ref, v_ref, z_ref, cwq_ref, cwk_ref, cwv_ref, gate_ref, nw_ref, o_ref,
                qn_ref, kn_ref, vn_ref, ob_ref, winq_ref, wink_ref, winv_ref,
                mqf_ref, nf_ref, decf_ref, mqb_ref, nb_ref, decb_ref, *, seq):
    n_chunk = seq // CHUNK
    c = CHUNK
    dir_refs = ((mqf_ref, nf_ref, decf_ref), (mqb_ref, nb_ref, decb_ref))

    cwq, cwk, cwv = cwq_ref[0], cwk_ref[0], cwv_ref[0]
    n_group = seq // GROUP_ROWS

    def conv_pieces(g):
        def piece(x_ref, cw, win_ref, dst_ref, normalise, scale, half):
            def run():
                t0 = pl.multiple_of(g * GROUP_ROWS + half * CONV_ROWS, CONV_ROWS)
                y = _conv_silu(x_ref, cw, win_ref, t0, seq)
                if normalise:
                    y = _l2n(y) * scale if scale != 1.0 else _l2n(y)
                dst_ref[pl.ds(t0, CONV_ROWS), :] = y.astype(BF16)
            return run
        streams = ((q_ref, cwq, winq_ref, qn_ref, True, HD ** -0.5), (k_ref, cwk, wink_ref, kn_ref, True, 1.0),
                   (v_ref, cwv, winv_ref, vn_ref, False, 1.0))
        return [piece(*s, half) for s in streams for half in range(GROUP_ROWS // CONV_ROWS)]

    ri = lax.broadcasted_iota(jnp.int32, (c, c), 0)
    ci = lax.broadcasted_iota(jnp.int32, (c, c), 1)
    eye = ri == ci
    eye_bf = jnp.where(eye, 1.0, 0.0).astype(BF16)

    def col(row):
        return jnp.sum(jnp.where(eye, row, 0.0), axis=1, keepdims=True)

    def inv_unit_tri(mats, fillers):
        ts = [jnp.where(eye, 1.0, 0.0) - jnp.where((ri // 2) == (ci // 2), a, 0.0) for a in mats]
        fillers = list(fillers)
        per_level = -(-len(fillers) // (c.bit_length() - 2))
        b = 2
        while b < c:
            off = ((ri // (2 * b)) == (ci // (2 * b))) & ((ri // b) != (ci // b))
            tbs = [t.astype(BF16) for t in ts]
            xs = [_dot(jnp.where(off, a, 0.0).astype(BF16), tb).astype(BF16) for a, tb in zip(mats, tbs)]
            for f in fillers[:per_level]:
                f()
            fillers = fillers[per_level:]
            ts = [t - _dot(tb, x) for t, tb, x in zip(ts, tbs, xs)]
            b *= 2
        return ts

    def prep_group(probs, first, fillers):
        t0s = [pl.multiple_of(ch * c, c) for ch, d in probs]
        qs = [qn_ref[pl.ds(t0, c), :] for t0 in t0s]
        ks = [kn_ref[pl.ds(t0, c), :] for t0 in t0s]
        vs = [vn_ref[pl.ds(t0, c), :] for t0 in t0s]
        gts = [gate_ref[:, pl.ds(t0, c)] for t0 in t0s]
        kqk = [_dot_nt(jnp.concatenate([k, q], axis=0), k) for q, k in zip(qs, ks)]
        grams = [x[:c] for x in kqk]
        qks = [x[c:] for x in kqk]
        kts = [k.astype(F32).T for k in ks]
        beta_r = [g[d:d + 1] for g, (ch, d) in zip(gts, probs)]
        gc_r = [g[2 + d:3 + d] for g, (ch, d) in zip(gts, probs)]
        gc_c = [col(g) for g in gc_r]
        beta_c = [col(bb) for bb in beta_r]
        masks = [(ri <= ci) if d else (ri >= ci) for ch, d in probs]
        stricts = [(ri < ci) if d else (ri > ci) for ch, d in probs]
        es = [jnp.where(m, jnp.exp(jnp.where(m, gcc - gcr, 0.0)), 0.0) for m, gcc, gcr in zip(masks, gc_c, gc_r)]
        amats = [jnp.where(st, gram * e * bc, 0.0) for gram, st, e, bc in zip(grams, stricts, es, beta_c)]
        ts = inv_unit_tri(amats, fillers)
        us = [_dot((t * br).astype(BF16), v) for t, br, v in zip(ts, beta_r, vs)]
        ws = [_dot((t * (br * jnp.exp(gr))).astype(BF16), k) for t, br, gr, k in zip(ts, beta_r, gc_r, ks)]
        wus = [jnp.concatenate([w, u], axis=1).astype(BF16) for w, u in zip(ws, us)]
        g_last = [gr[:, 0:1] if d else gr[:, c - 1:c] for (ch, d), gr in zip(probs, gc_r)]
        kes = [(kt * jnp.exp(gl - gr)).astype(BF16) for kt, gl, gr in zip(kts, g_last, gc_r)]
        qkms = [(qk * e).astype(BF16) for qk, e in zip(qks, es)]
        mq = [_dot(jnp.concatenate([ke, qkm], axis=0), wu) for ke, qkm, wu in zip(kes, qkms, wus)]
        for i, (ch, d) in enumerate(probs):
            mn, qo = mq[i][:c], mq[i][c:]
            mq_ref, n_ref, dec_ref = dir_refs[d]
            m0 = pl.multiple_of(ch * (2 * c), 2 * c)
            mq_ref[pl.ds(m0, c), :] = mn[:, :HD].astype(BF16)
            q_dec = qs[i].astype(F32) * jnp.exp(gc_c[i])
            mq_ref[pl.ds(m0 + c, c), :] = (q_dec - qo[:, :HD]).astype(BF16)
            n_ref[pl.ds(t0s[i], c), :] = mn[:, HD:]
            dec_ref[pl.ds(pl.multiple_of(ch * 8, 8), 8), :] = jnp.broadcast_to(jnp.exp(g_last[i]), (8, HD))
            if first:
                ob_ref[pl.ds(t0s[i], c), :] = qo[:, HD:]
            else:
                ob_ref[pl.ds(t0s[i], c), :] += qo[:, HD:]

    def step(chunk, s, refs):
        mq_ref, n_ref, dec_ref = refs
        t0 = pl.multiple_of(chunk * c, c)
        m0 = pl.multiple_of(chunk * (2 * c), 2 * c)
        s_bf = s.astype(BF16)
        ms = _dot(mq_ref[pl.ds(m0, c), :], s_bf)
        dec = dec_ref[pl.ds(pl.multiple_of(chunk * 8, 8), 8), :][0:1]
        ob_ref[pl.ds(t0, c), :] += _dot(mq_ref[pl.ds(m0 + c, c), :], s_bf)
        return s * dec - ms + n_ref[pl.ds(t0, c), :]

    half = n_group // 2

    def group_chunks(g, reverse=False):
        chunks = [g * PREP_CHUNKS + cc for cc in range(PREP_CHUNKS)]
        return chunks[::-1] if reverse else chunks

    def scan_steps(g, state):
        def one(d, chunk):
            def run():
                state[d] = step(chunk, state[d], dir_refs[d])
            return run
        pairs = zip(group_chunks(g), group_chunks(n_group - 1 - g, reverse=True))
        return [f for cf, cb in pairs for f in (one(0, cf), one(1, cb))]

    def next_convs(g):
        return conv_pieces(jnp.minimum(g + 1, half - 1)) + conv_pieces(jnp.maximum(n_group - 2 - g, half))

    def iteration(g, first, fillers):
        probs = [(ch, 0) for ch in group_chunks(g)] + [(ch, 1) for ch in group_chunks(n_group - 1 - g)]
        prep_group(probs, first, fillers)

    for run in conv_pieces(0) + conv_pieces(n_group - 1):
        run()
    iteration(0, True, next_convs(0))

    def first_half(g, carry):
        state = list(carry)
        iteration(g, True, scan_steps(g - 1, state) + next_convs(g))
        return tuple(state)

    nw = nw_ref[...]

    def norm_pieces(g):
        def piece(half_idx):
            def run():
                t0 = pl.multiple_of(g * GROUP_ROWS + half_idx * CONV_ROWS, CONV_ROWS)
                o = _rms(ob_ref[pl.ds(t0, CONV_ROWS), :], nw)
                zz = z_ref[pl.ds(t0, CONV_ROWS), :].astype(F32)
                o_ref[pl.ds(t0, CONV_ROWS), :] = (o * (zz * _sigmoid(zz))).astype(BF16)
            return run
        return [piece(i) for i in range(GROUP_ROWS // CONV_ROWS)]

    def second_half(g, carry):
        state = list(carry)
        iteration(g, False, scan_steps(g - 1, state))
        return tuple(state)

    def second_half_norm(g, carry):
        state = list(carry)
        iteration(g, False, scan_steps(g - 1, state) + norm_pieces(g - 2) + norm_pieces(n_group + 1 - g))
        return tuple(state)

    s0 = jnp.zeros((HD, HD), F32)
    norm_from = min(half + 2, n_group)
    carry = lax.fori_loop(1, half, first_half, (s0, s0))
    carry = lax.fori_loop(half, norm_from, second_half, carry)
    carry = lax.fori_loop(norm_from, n_group, second_half_norm, carry)
    state = list(carry)
    for run in scan_steps(n_group - 1, state):
        run()
    normed = {grp for g in range(norm_from, n_group) for grp in (g - 2, n_group + 1 - g)}
    for g in sorted(set(range(n_group)) - normed):
        for run in norm_pieces(g):
            run()


def _gdn(qkv, z, gates, P, B, T):
    nh = GDN_HEADS
    single = pl.Buffered(1)
    in_specs = [
        pl.BlockSpec((None, T, HD), lambda b, h: (b, 0, h), pipeline_mode=single),
        pl.BlockSpec((None, T, HD), lambda b, h: (b, 0, nh + h), pipeline_mode=single),
        pl.BlockSpec((None, T, HD), lambda b, h: (b, 0, 2 * nh + h), pipeline_mode=single),
        pl.BlockSpec((None, T, HD), lambda b, h: (b, 0, h), pipeline_mode=single),
        pl.BlockSpec((1, 8, HD), lambda b, h: (h, 0, 0)),
        pl.BlockSpec((1, 8, HD), lambda b, h: (nh + h, 0, 0)),
        pl.BlockSpec((1, 8, HD), lambda b, h: (2 * nh + h, 0, 0)),
        pl.BlockSpec((GATE_ROWS, T), lambda b, h: (h, b), pipeline_mode=single),
        pl.BlockSpec((1, HD), lambda b, h: (0, 0)),
    ]
    scratch = [
        pltpu.VMEM((T, HD), BF16), pltpu.VMEM((T, HD), BF16), pltpu.VMEM((T, HD), BF16),
        pltpu.VMEM((T, HD), F32),
        pltpu.VMEM((CONV_ROWS + 2 * CONV_HALO, HD), F32),
        pltpu.VMEM((CONV_ROWS + 2 * CONV_HALO, HD), F32),
        pltpu.VMEM((CONV_ROWS + 2 * CONV_HALO, HD), F32),
    ]
    per_direction = [pltpu.VMEM((2 * T, HD), BF16), pltpu.VMEM((T, HD), F32),
                     pltpu.VMEM((T // CHUNK * 8, HD), F32)]
    scratch = scratch + per_direction + per_direction
    return pl.pallas_call(
        functools.partial(_gdn_kernel, seq=T),
        grid=(B, nh), in_specs=in_specs,
        out_specs=pl.BlockSpec((None, T, HD), lambda b, h: (b, 0, h)),
        out_shape=jax.ShapeDtypeStruct((B, T, nh * HD), BF16),
        scratch_shapes=scratch,
        compiler_params=pltpu.CompilerParams(dimension_semantics=("parallel", "parallel"),
                                             vmem_limit_bytes=VMEM_LIMIT),
        name="gdn",
    )(qkv, qkv, qkv, z, P["conv_w"], P["conv_w"], P["conv_w"], gates, P["gdn_nw"])


def _attn_kernel(q_ref, k_ref, vt_ref, o_ref, m_ref, acc_ref, sta_ref, stb_ref, *, seq, tk):
    tq = q_ref.shape[0]
    q = jnp.concatenate([q_ref[:, g * HD:(g + 1) * HD] for g in range(ATT_GROUP)], axis=0)
    m_ref[...] = jnp.full_like(m_ref, -jnp.inf)
    acc_ref[...] = jnp.zeros_like(acc_ref)
    ones = jnp.ones((ATT_ONES_ROWS, tk), BF16)
    n_kv = seq // tk

    def scores(j, st_ref):
        t0 = pl.multiple_of(jnp.minimum(j, n_kv - 1) * tk, tk)
        st_ref[...] = _dot_nt(k_ref[pl.ds(t0, tk), :], q)

    def consume(j, st_ref):
        t0 = pl.multiple_of(j * tk, tk)
        m_old = m_ref[...]
        m_new = jnp.maximum(m_old, jnp.max(st_ref[...], axis=0, keepdims=True))
        pt = jnp.exp2(st_ref[...] - m_new).astype(BF16)
        alpha = jnp.exp2(m_old - m_new)
        vt = jnp.concatenate([vt_ref[:, pl.ds(t0, tk)], ones], axis=0)
        acc_ref[...] = alpha * acc_ref[...] + _dot(vt, pt)
        m_ref[...] = m_new

    scores(0, sta_ref)

    def body(i, carry):
        scores(2 * i + 1, stb_ref)
        consume(2 * i, sta_ref)
        scores(2 * i + 2, sta_ref)
        consume(2 * i + 1, stb_ref)
        return carry

    lax.fori_loop(0, n_kv // 2, body, 0)
    acc = acc_ref[...]
    out = (acc[:HD] / acc[HD:HD + 1]).T
    for g in range(ATT_GROUP):
        o_ref[:, g * HD:(g + 1) * HD] = out[g * tq:(g + 1) * tq].astype(BF16)


def _attn(q, k, vt, B, T, tq, tk):
    gw = ATT_GROUP * HD
    return pl.pallas_call(
        functools.partial(_attn_kernel, seq=T, tk=tk),
        grid=(B, ATT_KV_HEADS, T // tq),
        in_specs=[
            pl.BlockSpec((None, tq, gw), lambda b, h, i: (b, i, h)),
            pl.BlockSpec((None, T, HD), lambda b, h, i: (b, 0, h)),
            pl.BlockSpec((HD, T), lambda b, h, i: (h, b)),
        ],
        out_specs=pl.BlockSpec((None, tq, gw), lambda b, h, i: (b, i, h)),
        out_shape=jax.ShapeDtypeStruct((B, T, ATT_HEADS * HD), BF16),
        scratch_shapes=[pltpu.VMEM((1, ATT_GROUP * tq), F32),
                        pltpu.VMEM((HD + ATT_ONES_ROWS, ATT_GROUP * tq), F32),
                        pltpu.VMEM((tk, ATT_GROUP * tq), F32), pltpu.VMEM((tk, ATT_GROUP * tq), F32)],
        compiler_params=pltpu.CompilerParams(dimension_semantics=("parallel", "parallel", "parallel"),
                                             vmem_limit_bytes=VMEM_LIMIT),
        name="attn",
    )(q, k, vt)


def _out_mlp_kernel(x_ref, oa_ref, ob_ref, woa_ref, wob_ref, npost_ref, npre2_ref, wup_ref, wdn_ref,
                    npost2_ref, y_ref, *, ff_chunk):
    mix = _dot(oa_ref[...], woa_ref[...]) + _dot(ob_ref[...], wob_ref[...])
    x1 = x_ref[...] + _rms(mix, npost_ref[...])
    hm = _rms(x1, npre2_ref[...]).astype(BF16)
    d_ff = wup_ref.shape[1]
    f = jnp.zeros(x1.shape, F32)
    for cc in range(d_ff // ff_chunk):
        hc = jnp.maximum(_dot(hm, wup_ref[:, cc * ff_chunk:(cc + 1) * ff_chunk]), 0.0)
        f = f + _dot((hc * hc).astype(BF16), wdn_ref[cc * ff_chunk:(cc + 1) * ff_chunk, :])
    y_ref[...] = x1 + _rms(f, npost2_ref[...])


def _out_mlp(x2, oa, ob, P, tm, ff_chunk):
    n_tok, d = x2.shape
    const = lambda i: (0, 0)
    tile = lambda i: (i, 0)
    single = pl.Buffered(1)

    def full(a):
        return pl.BlockSpec(a.shape, const, pipeline_mode=single)

    return pl.pallas_call(
        functools.partial(_out_mlp_kernel, ff_chunk=ff_chunk),
        grid=(n_tok // tm,),
        in_specs=[pl.BlockSpec((tm, d), tile), pl.BlockSpec((tm, oa.shape[1]), tile),
                  pl.BlockSpec((tm, ob.shape[1]), tile),
                  full(P["w_out_a"]), full(P["w_out_b"]), full(P["nw_post"]), full(P["nw_pre2"]),
                  full(P["w_up"]), full(P["w_down"]), full(P["nw_post2"])],
        out_specs=pl.BlockSpec((tm, d), tile),
        out_shape=jax.ShapeDtypeStruct((n_tok, d), F32),
        compiler_params=pltpu.CompilerParams(dimension_semantics=("parallel",), vmem_limit_bytes=VMEM_LIMIT),
        name="out_mlp",
    )(x2, oa, ob, P["w_out_a"], P["w_out_b"], P["nw_post"], P["nw_pre2"], P["w_up"], P["w_down"],
      P["nw_post2"])


def _rope_tables(T):
    n_freq = HD // 4
    t = np.arange(T)
    inv_freq = ROPE_THETA ** (-np.arange(n_freq, dtype=np.float32) / n_freq)
    row_ang = jnp.asarray((t // GRID_W).astype(np.float32))[:, None] * jnp.asarray(inv_freq)[None, :]
    col_ang = jnp.asarray((t % GRID_W).astype(np.float32))[:, None] * jnp.asarray(inv_freq)[None, :]
    cr, sr, cc, sc = jnp.cos(row_ang), jnp.sin(row_ang), jnp.cos(col_ang), jnp.sin(col_ang)
    return jnp.concatenate([cr, cc, cr, cc], axis=1), jnp.concatenate([-sr, -sc, sr, sc], axis=1)


def _chunk_cumsum_mats(tm):
    t = np.arange(tm)
    same = (t[:, None] // CHUNK) == (t[None, :] // CHUNK)
    fwd = same & (t[:, None] <= t[None, :])
    bwd = same & (t[:, None] >= t[None, :])
    return jnp.asarray(fwd, BF16), jnp.asarray(bwd, BF16)


def _prepare(norm_mix_pre, w_in, conv_w, A_log_f, A_log_b, dt_bias_f, dt_bias_b, gdn_norm_w, q_norm_w,
             k_norm_w, w_out, norm_mix_post, norm_mlp_pre, w_up, w_down, norm_mlp_post, seqs):
    gw = GDN_HEADS * HD
    sizes = (3 * gw, gw, GDN_HEADS, GDN_HEADS, GDN_HEADS, GDN_HEADS, ATT_HEADS * HD, ATT_KV_HEADS * HD,
             ATT_KV_HEADS * HD)
    offs = np.concatenate([[0], np.cumsum(sizes)])
    cols = [w_in[:, offs[i]:offs[i + 1]] for i in range(len(sizes))]
    w_qkv, w_z, w_bf, w_bb, w_af, w_ab, w_q, w_k, w_v = cols
    d = w_in.shape[0]
    perm = np.concatenate([np.arange(0, 32), np.arange(64, 96), np.arange(32, 64), np.arange(96, 128)])

    def permute_heads(w, n):
        return w.reshape(d, n, HD)[:, :, perm].reshape(d, n * HD)

    zeros = jnp.zeros((d, GDN_HEADS), F32)
    w_g = jnp.stack([w_bf, w_bb, w_af, w_ab, zeros, zeros, zeros, zeros], axis=2)
    w_g = w_g.reshape(d, GDN_HEADS * GATE_ROWS).T
    zh = jnp.zeros((GDN_HEADS,), F32)
    alog_rows = jnp.stack([zh, zh, A_log_f, A_log_b, zh, zh, zh, zh], axis=1).reshape(-1, 1)
    dtb_rows = jnp.stack([zh, zh, dt_bias_f, dt_bias_b, zh, zh, zh, zh], axis=1).reshape(-1, 1)
    cum_f, cum_b = _chunk_cumsum_mats(IN_SUB)
    cw = jnp.pad(conv_w, ((0, 8 - CONV_K), (0, 0)))
    cw = cw.reshape(8, 3 * GDN_HEADS, HD).transpose(1, 0, 2)
    row = lambda a: a.reshape(1, -1).astype(F32)
    return {
        "nw_pre": row(norm_mix_pre),
        "w_qkv": w_qkv.astype(BF16), "w_z": w_z.astype(BF16), "w_g": w_g.astype(BF16),
        "w_q": permute_heads(w_q, ATT_HEADS).astype(BF16),
        "w_k": permute_heads(w_k, ATT_KV_HEADS).astype(BF16),
        "w_v": w_v.T.astype(BF16),
        "alog_rows": alog_rows, "dtb_rows": dtb_rows, "cum_f": cum_f, "cum_b": cum_b,
        "rope": {T: _rope_tables(T) for T in seqs},
        "qnw": row(q_norm_w[perm]), "knw": row(k_norm_w[perm]),
        "conv_w": cw, "gdn_nw": row(gdn_norm_w),
        "w_out_a": w_out[:gw].astype(BF16), "w_out_b": w_out[gw:].astype(BF16),
        "nw_post": row(norm_mix_post), "nw_pre2": row(norm_mlp_pre),
        "w_up": w_up.astype(BF16), "w_down": w_down.astype(BF16), "nw_post2": row(norm_mlp_post),
    }


IN_TILE = 1024
MLP_TILE = 512
ATT_Q_TILE = 1024
ATT_K_TILE = 1024
FF_CHUNK = 1024


def _layer(x, P):
    B, T, d = x.shape
    x2 = x.reshape(B * T, d)
    qkv, z, gates, q, k, vt = _in_proj(x2, T, P, min(IN_TILE, T))
    o_a = _gdn(qkv.reshape(B, T, -1), z.reshape(B, T, -1), gates, P, B, T)
    o_b = _attn(q.reshape(B, T, -1), k.reshape(B, T, -1), vt, B, T, min(ATT_Q_TILE, T), min(ATT_K_TILE, T // 2))
    y = _out_mlp(x2, o_a.reshape(B * T, -1), o_b.reshape(B * T, -1), P, min(MLP_TILE, T), FF_CHUNK)
    return y.reshape(B, T, d)


def kernel(x_prompt, x_sample, norm_mix_pre, w_in, conv_w, A_log_f, A_log_b, dt_bias_f, dt_bias_b,
           gdn_norm_w, q_norm_w, k_norm_w, w_out, norm_mix_post, norm_mlp_pre, w_up, w_down,
           norm_mlp_post):
    y_prompt, y_sample = x_prompt, x_sample
    for l in range(w_in.shape[0]):
        P = _prepare(norm_mix_pre[l], w_in[l], conv_w[l], A_log_f[l], A_log_b[l], dt_bias_f[l],
                     dt_bias_b[l], gdn_norm_w[l], q_norm_w[l], k_norm_w[l], w_out[l], norm_mix_post[l],
                     norm_mlp_pre[l], w_up[l], w_down[l], norm_mlp_post[l],
                     seqs={y_prompt.shape[1], y_sample.shape[1]})
        y_prompt = _layer(y_prompt, P)
        y_sample = _layer(y_sample, P)
    return (y_prompt, y_sample)
```

```python
import functools
import math

import jax
import jax.numpy as jnp
import numpy as np
from jax import lax
from jax.experimental import pallas as pl
from jax.experimental.pallas import tpu as pltpu

F32 = jnp.float32
BF16 = jnp.bfloat16

EPS = 1e-6
GRID_W = 64
ROPE_THETA = 10000.0
CONV_K = 5
HD = 128
GDN_HEADS = 4
ATT_HEADS = 4
ATT_KV_HEADS = 2
ATT_GROUP = ATT_HEADS // ATT_KV_HEADS
LANES = 128
CHUNK = LANES
GATE_ROWS = 8
VMEM_LIMIT = 56 * 1024 * 1024
IN_SUB = 256
ATT_ONES_ROWS = 16

_NT = (((1,), (1,)), ((), ()))


def _dot(a, b):
    return jnp.dot(a, b, preferred_element_type=F32)


def _dot_nt(a, b):
    return lax.dot_general(a, b, _NT, preferred_element_type=F32)


def _rms(x, w):
    return x * lax.rsqrt(jnp.mean(x * x, axis=-1, keepdims=True) + EPS) * w


def _sigmoid(x):
    return 1.0 / (1.0 + jnp.exp(-x))


def _in_proj_kernel(x_ref, nw_ref, wqkv_ref, wz_ref, wg_ref, wq_ref, wk_ref, wv_ref,
                    alog_ref, dtb_ref, cumf_ref, cumb_ref, cos_ref, sin_ref, qnw_ref, knw_ref,
                    qkv_ref, z_ref, gate_ref, q_ref, k_ref, vt_ref, *, q_scale):
    for s in range(x_ref.shape[0] // IN_SUB):
        r = slice(s * IN_SUB, (s + 1) * IN_SUB)
        h = _rms(x_ref[r, :], nw_ref[...]).astype(BF16)

        qkv_ref[r, :] = _dot(h, wqkv_ref[...]).astype(BF16)
        z_ref[r, :] = _dot(h, wz_ref[...]).astype(BF16)
        vt_ref[:, r] = _dot_nt(wv_ref[...], h).astype(BF16)

        graw = _dot_nt(wg_ref[...], h)
        kind = lax.broadcasted_iota(jnp.int32, graw.shape, 0) % GATE_ROWS
        beta = _sigmoid(graw)
        a = graw + dtb_ref[...]
        softplus = jnp.maximum(a, 0.0) + jnp.log1p(jnp.exp(-jnp.abs(a)))
        g = -jnp.exp(alog_ref[...]) * softplus
        g = jnp.where(kind >= 2, g, 0.0)
        p1 = g.astype(BF16)
        r1 = g - p1.astype(F32)
        p2 = r1.astype(BF16)
        p3 = (r1 - p2.astype(F32)).astype(BF16)
        nrow = g.shape[0]
        parts = jnp.concatenate([p1, p2, p3], axis=0)
        cf = _dot(parts, cumf_ref[...])
        cb = _dot(parts, cumb_ref[...])
        cf = cf[:nrow] + cf[nrow:2 * nrow] + cf[2 * nrow:]
        cb = cb[:nrow] + cb[nrow:2 * nrow] + cb[2 * nrow:]
        gate_ref[:, r] = jnp.where(kind < 2, beta, jnp.where(kind == 2, cf, jnp.where(kind == 3, cb, 0.0)))

        cos = cos_ref[r, :]
        sin = sin_ref[r, :]

        def norm_rope(y, w, scale):
            y = _rms(y, w)
            y = y * cos + pltpu.roll(y, HD // 2, axis=1) * sin
            return (y * scale).astype(BF16) if scale != 1.0 else y.astype(BF16)

        qa = _dot(h, wq_ref[...])
        for hh in range(ATT_HEADS):
            q_ref[r, hh * HD:(hh + 1) * HD] = norm_rope(qa[:, hh * HD:(hh + 1) * HD], qnw_ref[...], q_scale)
        ka = _dot(h, wk_ref[...])
        for hh in range(ATT_KV_HEADS):
            k_ref[r, hh * HD:(hh + 1) * HD] = norm_rope(ka[:, hh * HD:(hh + 1) * HD], knw_ref[...], 1.0)


def _in_proj(x2, T, P, tm):
    n_tok, d = x2.shape
    nt = n_tok // tm
    tpb = T // tm
    const = lambda i: (0, 0)
    tile = lambda i: (i, 0)

    def full(a):
        return pl.BlockSpec(a.shape, const)

    in_specs = [
        pl.BlockSpec((tm, d), tile), full(P["nw_pre"]),
        full(P["w_qkv"]), full(P["w_z"]), full(P["w_g"]), full(P["w_q"]), full(P["w_k"]), full(P["w_v"]),
        full(P["alog_rows"]), full(P["dtb_rows"]), full(P["cum_f"]), full(P["cum_b"]),
        pl.BlockSpec((tm, HD), lambda i: (i % tpb, 0)), pl.BlockSpec((tm, HD), lambda i: (i % tpb, 0)),
        full(P["qnw"]), full(P["knw"]),
    ]
    n_gate = GDN_HEADS * GATE_ROWS
    out_shape = [
        jax.ShapeDtypeStruct((n_tok, 3 * GDN_HEADS * HD), BF16),
        jax.ShapeDtypeStruct((n_tok, GDN_HEADS * HD), BF16),
        jax.ShapeDtypeStruct((n_gate, n_tok), F32),
        jax.ShapeDtypeStruct((n_tok, ATT_HEADS * HD), BF16),
        jax.ShapeDtypeStruct((n_tok, ATT_KV_HEADS * HD), BF16),
        jax.ShapeDtypeStruct((ATT_KV_HEADS * HD, n_tok), BF16),
    ]
    out_specs = [
        pl.BlockSpec((tm, 3 * GDN_HEADS * HD), tile),
        pl.BlockSpec((tm, GDN_HEADS * HD), tile),
        pl.BlockSpec((n_gate, tm), lambda i: (0, i)),
        pl.BlockSpec((tm, ATT_HEADS * HD), tile),
        pl.BlockSpec((tm, ATT_KV_HEADS * HD), tile),
        pl.BlockSpec((ATT_KV_HEADS * HD, tm), lambda i: (0, i)),
    ]
    q_scale = (HD ** -0.5) * math.log2(math.e)
    cos, sin = P["rope"][T]
    return pl.pallas_call(
        functools.partial(_in_proj_kernel, q_scale=q_scale),
        grid=(nt,), in_specs=in_specs, out_specs=out_specs, out_shape=out_shape,
        compiler_params=pltpu.CompilerParams(dimension_semantics=("parallel",), vmem_limit_bytes=VMEM_LIMIT),
        name="in_proj",
    )(x2, P["nw_pre"], P["w_qkv"], P["w_z"], P["w_g"], P["w_q"], P["w_k"], P["w_v"],
      P["alog_rows"], P["dtb_rows"], P["cum_f"], P["cum_b"], cos, sin, P["qnw"], P["knw"])


PREP_CHUNKS = 4
GROUP_ROWS = PREP_CHUNKS * CHUNK
CONV_ROWS = GROUP_ROWS // 2
CONV_HALO = 16


def _conv_silu(x_ref, cw, win_ref, t0, seq):
    r, hl = CONV_ROWS, CONV_HALO
    prev = x_ref[pl.ds(pl.multiple_of(jnp.maximum(t0 - hl, 0), hl), hl), :].astype(F32)
    nxt = x_ref[pl.ds(pl.multiple_of(jnp.minimum(t0 + r, seq - hl), hl), hl), :].astype(F32)
    win_ref[0:hl, :] = jnp.where(t0 > 0, prev, 0.0)
    win_ref[hl:hl + r, :] = x_ref[pl.ds(t0, r), :].astype(F32)
    win_ref[hl + r:hl + r + hl, :] = jnp.where(t0 + r < seq, nxt, 0.0)
    acc = jnp.zeros((r, LANES), F32)
    for j in range(CONV_K):
        off = hl - CONV_K // 2 + j
        acc = acc + cw[j:j + 1, :] * win_ref[off:off + r, :]
    return acc * _sigmoid(acc)


def _l2n(x):
    return x * lax.rsqrt(jnp.sum(x * x, axis=-1, keepdims=True) + EPS)


def _gdn_kernel(q_ref, k_ref, v_ref, z_ref, cwq_ref, cwk_ref, cwv_ref, gate_ref, nw_ref, o_ref,
                qn_ref, kn_ref, vn_ref, ob_ref, winq_ref, wink_ref, winv_ref,
                mqf_ref, nf_ref, decf_ref, mqb_ref, nb_ref, decb_ref, *, seq):
    c = CHUNK
    dir_refs = ((mqf_ref, nf_ref, decf_ref), (mqb_ref, nb_ref, decb_ref))

    cwq, cwk, cwv = cwq_ref[0], cwk_ref[0], cwv_ref[0]
    n_group = seq // GROUP_ROWS

    def conv_pieces(g):
        def piece(x_ref, cw, win_ref, dst_ref, normalise, scale, half):
            def run():
                t0 = pl.multiple_of(g * GROUP_ROWS + half * CONV_ROWS, CONV_ROWS)
                y = _conv_silu(x_ref, cw, win_ref, t0, seq)
                if normalise:
                    y = _l2n(y) * scale if scale != 1.0 else _l2n(y)
                dst_ref[pl.ds(t0, CONV_ROWS), :] = y.astype(BF16)
            return run
        streams = ((q_ref, cwq, winq_ref, qn_ref, True, HD ** -0.5), (k_ref, cwk, wink_ref, kn_ref, True, 1.0),
                   (v_ref, cwv, winv_ref, vn_ref, False, 1.0))
        return [piece(*s, half) for s in streams for half in range(GROUP_ROWS // CONV_ROWS)]

    ri = lax.broadcasted_iota(jnp.int32, (c, c), 0)
    ci = lax.broadcasted_iota(jnp.int32, (c, c), 1)
    eye = ri == ci

    def col(row):
        return jnp.sum(jnp.where(eye, row, 0.0), axis=1, keepdims=True)

    def inv_unit_tri(mats, fillers):
        ts = [jnp.where(eye, 1.0, 0.0) - jnp.where((ri // 2) == (ci // 2), a, 0.0) for a in mats]
        fillers = list(fillers)
        per_level = -(-len(fillers) // (c.bit_length() - 2))
        b = 2
        while b < c:
            off = ((ri // (2 * b)) == (ci // (2 * b))) & ((ri // b) != (ci // b))
            tbs = [t.astype(BF16) for t in ts]
            xs = [_dot(jnp.where(off, a, 0.0).astype(BF16), tb).astype(BF16) for a, tb in zip(mats, tbs)]
            for f in fillers[:per_level]:
                f()
            fillers = fillers[per_level:]
            ts = [t - _dot(tb, x) for t, tb, x in zip(ts, tbs, xs)]
            b *= 2
        return ts

    def prep_group(probs, first, fillers):
        t0s = [pl.multiple_of(ch * c, c) for ch, d in probs]
        qs = [qn_ref[pl.ds(t0, c), :] for t0 in t0s]
        ks = [kn_ref[pl.ds(t0, c), :] for t0 in t0s]
        vs = [vn_ref[pl.ds(t0, c), :] for t0 in t0s]
        gts = [gate_ref[:, pl.ds(t0, c)] for t0 in t0s]
        kqk = [_dot_nt(jnp.concatenate([k, q], axis=0), k) for q, k in zip(qs, ks)]
        grams = [x[:c] for x in kqk]
        qks = [x[c:] for x in kqk]
        kts = [k.astype(F32).T for k in ks]
        beta_r = [g[d:d + 1] for g, (ch, d) in zip(gts, probs)]
        gc_r = [g[2 + d:3 + d] for g, (ch, d) in zip(gts, probs)]
        gc_c = [col(g) for g in gc_r]
        beta_c = [col(bb) for bb in beta_r]
        masks = [(ri <= ci) if d else (ri >= ci) for ch, d in probs]
        stricts = [(ri < ci) if d else (ri > ci) for ch, d in probs]
        es = [jnp.where(m, jnp.exp(jnp.where(m, gcc - gcr, 0.0)), 0.0) for m, gcc, gcr in zip(masks, gc_c, gc_r)]
        amats = [jnp.where(st, gram * e * bc, 0.0) for gram, st, e, bc in zip(grams, stricts, es, beta_c)]
        ts = inv_unit_tri(amats, fillers)
        us = [_dot((t * br).astype(BF16), v) for t, br, v in zip(ts, beta_r, vs)]
        ws = [_dot((t * (br * jnp.exp(gr))).astype(BF16), k) for t, br, gr, k in zip(ts, beta_r, gc_r, ks)]
        wus = [jnp.concatenate([w, u], axis=1).astype(BF16) for w, u in zip(ws, us)]
        g_last = [gr[:, 0:1] if d else gr[:, c - 1:c] for (ch, d), gr in zip(probs, gc_r)]
        kes = [(kt * jnp.exp(gl - gr)).astype(BF16) for kt, gl, gr in zip(kts, g_last, gc_r)]
        qkms = [(qk * e).astype(BF16) for qk, e in zip(qks, es)]
        mq = [_dot(jnp.concatenate([ke, qkm], axis=0), wu) for ke, qkm, wu in zip(kes, qkms, wus)]
        for i, (ch, d) in enumerate(probs):
            mn, qo = mq[i][:c], mq[i][c:]
            mq_ref, n_ref, dec_ref = dir_refs[d]
            m0 = pl.multiple_of(ch * (2 * c), 2 * c)
            mq_ref[pl.ds(m0, c), :] = mn[:, :HD].astype(BF16)
            q_dec = qs[i].astype(F32) * jnp.exp(gc_c[i])
            mq_ref[pl.ds(m0 + c, c), :] = (q_dec - qo[:, :HD]).astype(BF16)
            n_ref[pl.ds(t0s[i], c), :] = mn[:, HD:]
            dec_ref[pl.ds(pl.multiple_of(ch * 8, 8), 8), :] = jnp.broadcast_to(jnp.exp(g_last[i]), (8, HD))
            if first:
                ob_ref[pl.ds(t0s[i], c), :] = qo[:, HD:]
            else:
                ob_ref[pl.ds(t0s[i], c), :] += qo[:, HD:]

    def step(chunk, s, refs):
        mq_ref, n_ref, dec_ref = refs
        t0 = pl.multiple_of(chunk * c, c)
        m0 = pl.multiple_of(chunk * (2 * c), 2 * c)
        s_bf = s.astype(BF16)
        ms = _dot(mq_ref[pl.ds(m0, c), :], s_bf)
        dec = dec_ref[pl.ds(pl.multiple_of(chunk * 8, 8), 8), :][0:1]
        ob_ref[pl.ds(t0, c), :] += _dot(mq_ref[pl.ds(m0 + c, c), :], s_bf)
        return s * dec - ms + n_ref[pl.ds(t0, c), :]

    half = n_group // 2

    def group_chunks(g, reverse=False):
        chunks = [g * PREP_CHUNKS + cc for cc in range(PREP_CHUNKS)]
        return chunks[::-1] if reverse else chunks

    def scan_steps(g, state):
        def one(d, chunk):
            def run():
                state[d] = step(chunk, state[d], dir_refs[d])
            return run
        pairs = zip(group_chunks(g), group_chunks(n_group - 1 - g, reverse=True))
        return [f for cf, cb in pairs for f in (one(0, cf), one(1, cb))]

    def next_convs(g):
        return conv_pieces(jnp.minimum(g + 1, half - 1)) + conv_pieces(jnp.maximum(n_group - 2 - g, half))

    def iteration(g, first, fillers):
        probs = [(ch, 0) for ch in group_chunks(g)] + [(ch, 1) for ch in group_chunks(n_group - 1 - g)]
        prep_group(probs, first, fillers)

    for run in conv_pieces(0) + conv_pieces(n_group - 1):
        run()
    iteration(0, True, next_convs(0))

    def first_half(g, carry):
        state = list(carry)
        iteration(g, True, scan_steps(g - 1, state) + next_convs(g))
        return tuple(state)

    nw = nw_ref[...]

    def norm_pieces(g):
        def piece(half_idx):
            def run():
                t0 = pl.multiple_of(g * GROUP_ROWS + half_idx * CONV_ROWS, CONV_ROWS)
                o = _rms(ob_ref[pl.ds(t0, CONV_ROWS), :], nw)
                zz = z_ref[pl.ds(t0, CONV_ROWS), :].astype(F32)
                o_ref[pl.ds(t0, CONV_ROWS), :] = (o * (zz * _sigmoid(zz))).astype(BF16)
            return run
        return [piece(i) for i in range(GROUP_ROWS // CONV_ROWS)]

    def second_half(g, carry):
        state = list(carry)
        iteration(g, False, scan_steps(g - 1, state))
        return tuple(state)

    def second_half_norm(g, carry):
        state = list(carry)
        iteration(g, False, scan_steps(g - 1, state) + norm_pieces(g - 2) + norm_pieces(n_group + 1 - g))
        return tuple(state)

    s0 = jnp.zeros((HD, HD), F32)
    norm_from = min(half + 2, n_group)
    carry = lax.fori_loop(1, half, first_half, (s0, s0))
    carry = lax.fori_loop(half, norm_from, second_half, carry)
    carry = lax.fori_loop(norm_from, n_group, second_half_norm, carry)
    state = list(carry)
    for run in scan_steps(n_group - 1, state):
        run()
    normed = {grp for g in range(norm_from, n_group) for grp in (g - 2, n_group + 1 - g)}
    for g in sorted(set(range(n_group)) - normed):
        for run in norm_pieces(g):
            run()


def _gdn(qkv, z, gates, P, B, T):
    nh = GDN_HEADS
    in_specs = [
        pl.BlockSpec((None, T, HD), lambda b, h: (b, 0, h)),
        pl.BlockSpec((None, T, HD), lambda b, h: (b, 0, nh + h)),
        pl.BlockSpec((None, T, HD), lambda b, h: (b, 0, 2 * nh + h)),
        pl.BlockSpec((None, T, HD), lambda b, h: (b, 0, h)),
        pl.BlockSpec((1, 8, HD), lambda b, h: (h, 0, 0)),
        pl.BlockSpec((1, 8, HD), lambda b, h: (nh + h, 0, 0)),
        pl.BlockSpec((1, 8, HD), lambda b, h: (2 * nh + h, 0, 0)),
        pl.BlockSpec((GATE_ROWS, T), lambda b, h: (h, b)),
        pl.BlockSpec((1, HD), lambda b, h: (0, 0)),
    ]
    scratch = [
        pltpu.VMEM((T, HD), BF16), pltpu.VMEM((T, HD), BF16), pltpu.VMEM((T, HD), BF16),
        pltpu.VMEM((T, HD), F32),
        pltpu.VMEM((CONV_ROWS + 2 * CONV_HALO, HD), F32),
        pltpu.VMEM((CONV_ROWS + 2 * CONV_HALO, HD), F32),
        pltpu.VMEM((CONV_ROWS + 2 * CONV_HALO, HD), F32),
    ]
    per_direction = [pltpu.VMEM((2 * T, HD), BF16), pltpu.VMEM((T, HD), F32),
                     pltpu.VMEM((T // CHUNK * 8, HD), F32)]
    scratch = scratch + per_direction + per_direction
    return pl.pallas_call(
        functools.partial(_gdn_kernel, seq=T),
        grid=(B, nh), in_specs=in_specs,
        out_specs=pl.BlockSpec((None, T, HD), lambda b, h: (b, 0, h)),
        out_shape=jax.ShapeDtypeStruct((B, T, nh * HD), BF16),
        scratch_shapes=scratch,
        compiler_params=pltpu.CompilerParams(dimension_semantics=("parallel", "parallel"),
                                             vmem_limit_bytes=VMEM_LIMIT),
        name="gdn",
    )(qkv, qkv, qkv, z, P["conv_w"], P["conv_w"], P["conv_w"], gates, P["gdn_nw"])


def _attn_kernel(q_ref, k_ref, vt_ref, o_ref, m_ref, acc_ref, sta_ref, stb_ref, *, seq, tk):
    tq = q_ref.shape[0]
    q = jnp.concatenate([q_ref[:, g * HD:(g + 1) * HD] for g in range(ATT_GROUP)], axis=0)
    m_ref[...] = jnp.full_like(m_ref, -jnp.inf)
    acc_ref[...] = jnp.zeros_like(acc_ref)
    ones = jnp.ones((ATT_ONES_ROWS, tk), BF16)
    n_kv = seq // tk

    def scores(j, st_ref):
        t0 = pl.multiple_of(jnp.minimum(j, n_kv - 1) * tk, tk)
        st_ref[...] = _dot_nt(k_ref[pl.ds(t0, tk), :], q)

    def consume(j, st_ref):
        t0 = pl.multiple_of(j * tk, tk)
        m_old = m_ref[...]
        m_new = jnp.maximum(m_old, jnp.max(st_ref[...], axis=0, keepdims=True))
        pt = jnp.exp2(st_ref[...] - m_new).astype(BF16)
        alpha = jnp.exp2(m_old - m_new)
        vt = jnp.concatenate([vt_ref[:, pl.ds(t0, tk)], ones], axis=0)
        acc_ref[...] = alpha * acc_ref[...] + _dot(vt, pt)
        m_ref[...] = m_new

    scores(0, sta_ref)

    def body(i, carry):
        scores(2 * i + 1, stb_ref)
        consume(2 * i, sta_ref)
        scores(2 * i + 2, sta_ref)
        consume(2 * i + 1, stb_ref)
        return carry

    lax.fori_loop(0, n_kv // 2, body, 0)
    acc = acc_ref[...]
    out = (acc[:HD] / acc[HD:HD + 1]).T
    for g in range(ATT_GROUP):
        o_ref[:, g * HD:(g + 1) * HD] = out[g * tq:(g + 1) * tq].astype(BF16)


def _attn(q, k, vt, B, T, tq, tk):
    gw = ATT_GROUP * HD
    return pl.pallas_call(
        functools.partial(_attn_kernel, seq=T, tk=tk),
        grid=(B, ATT_KV_HEADS, T // tq),
        in_specs=[
            pl.BlockSpec((None, tq, gw), lambda b, h, i: (b, i, h)),
            pl.BlockSpec((None, T, HD), lambda b, h, i: (b, 0, h)),
            pl.BlockSpec((HD, T), lambda b, h, i: (h, b)),
        ],
        out_specs=pl.BlockSpec((None, tq, gw), lambda b, h, i: (b, i, h)),
        out_shape=jax.ShapeDtypeStruct((B, T, ATT_HEADS * HD), BF16),
        scratch_shapes=[pltpu.VMEM((1, ATT_GROUP * tq), F32),
                        pltpu.VMEM((HD + ATT_ONES_ROWS, ATT_GROUP * tq), F32),
                        pltpu.VMEM((tk, ATT_GROUP * tq), F32), pltpu.VMEM((tk, ATT_GROUP * tq), F32)],
        compiler_params=pltpu.CompilerParams(dimension_semantics=("parallel", "parallel", "parallel"),
                                             vmem_limit_bytes=VMEM_LIMIT),
        name="attn",
    )(q, k, vt)


def _out_mlp_kernel(x_ref, oa_ref, ob_ref, woa_ref, wob_ref, npost_ref, npre2_ref, wup_ref, wdn_ref,
                    npost2_ref, y_ref, *, ff_chunk):
    mix = _dot(oa_ref[...], woa_ref[...]) + _dot(ob_ref[...], wob_ref[...])
    x1 = x_ref[...] + _rms(mix, npost_ref[...])
    hm = _rms(x1, npre2_ref[...]).astype(BF16)
    d_ff = wup_ref.shape[1]
    f = jnp.zeros(x1.shape, F32)
    for cc in range(d_ff // ff_chunk):
        hc = jnp.maximum(_dot(hm, wup_ref[:, cc * ff_chunk:(cc + 1) * ff_chunk]), 0.0)
        f = f + _dot((hc * hc).astype(BF16), wdn_ref[cc * ff_chunk:(cc + 1) * ff_chunk, :])
    y_ref[...] = x1 + _rms(f, npost2_ref[...])


def _out_mlp(x2, oa, ob, P, tm, ff_chunk):
    n_tok, d = x2.shape
    const = lambda i: (0, 0)
    tile = lambda i: (i, 0)
    single = pl.Buffered(1)

    def full(a):
        return pl.BlockSpec(a.shape, const, pipeline_mode=single)

    return pl.pallas_call(
        functools.partial(_out_mlp_kernel, ff_chunk=ff_chunk),
        grid=(n_tok // tm,),
        in_specs=[pl.BlockSpec((tm, d), tile), pl.BlockSpec((tm, oa.shape[1]), tile),
                  pl.BlockSpec((tm, ob.shape[1]), tile),
                  full(P["w_out_a"]), full(P["w_out_b"]), full(P["nw_post"]), full(P["nw_pre2"]),
                  full(P["w_up"]), full(P["w_down"]), full(P["nw_post2"])],
        out_specs=pl.BlockSpec((tm, d), tile),
        out_shape=jax.ShapeDtypeStruct((n_tok, d), F32),
        compiler_params=pltpu.CompilerParams(dimension_semantics=("parallel",), vmem_limit_bytes=VMEM_LIMIT),
        name="out_mlp",
    )(x2, oa, ob, P["w_out_a"], P["w_out_b"], P["nw_post"], P["nw_pre2"], P["w_up"], P["w_down"],
      P["nw_post2"])


def _rope_tables(T):
    n_freq = HD // 4
    t = np.arange(T)
    inv_freq = ROPE_THETA ** (-np.arange(n_freq, dtype=np.float32) / n_freq)
    row_ang = jnp.asarray((t // GRID_W).astype(np.float32))[:, None] * jnp.asarray(inv_freq)[None, :]
    col_ang = jnp.asarray((t % GRID_W).astype(np.float32))[:, None] * jnp.asarray(inv_freq)[None, :]
    cr, sr, cc, sc = jnp.cos(row_ang), jnp.sin(row_ang), jnp.cos(col_ang), jnp.sin(col_ang)
    return jnp.concatenate([cr, cc, cr, cc], axis=1), jnp.concatenate([-sr, -sc, sr, sc], axis=1)


def _chunk_cumsum_mats(tm):
    t = np.arange(tm)
    same = (t[:, None] // CHUNK) == (t[None, :] // CHUNK)
    fwd = same & (t[:, None] <= t[None, :])
    bwd = same & (t[:, None] >= t[None, :])
    return jnp.asarray(fwd, BF16), jnp.asarray(bwd, BF16)


def _prepare(norm_mix_pre, w_in, conv_w, A_log_f, A_log_b, dt_bias_f, dt_bias_b, gdn_norm_w, q_norm_w,
             k_norm_w, w_out, norm_mix_post, norm_mlp_pre, w_up, w_down, norm_mlp_post, seqs):
    gw = GDN_HEADS * HD
    sizes = (3 * gw, gw, GDN_HEADS, GDN_HEADS, GDN_HEADS, GDN_HEADS, ATT_HEADS * HD, ATT_KV_HEADS * HD,
             ATT_KV_HEADS * HD)
    offs = np.concatenate([[0], np.cumsum(sizes)])
    cols = [w_in[:, offs[i]:offs[i + 1]] for i in range(len(sizes))]
    w_qkv, w_z, w_bf, w_bb, w_af, w_ab, w_q, w_k, w_v = cols
    d = w_in.shape[0]
    perm = np.concatenate([np.arange(0, 32), np.arange(64, 96), np.arange(32, 64), np.arange(96, 128)])

    def permute_heads(w, n):
        return w.reshape(d, n, HD)[:, :, perm].reshape(d, n * HD)

    zeros = jnp.zeros((d, GDN_HEADS), F32)
    w_g = jnp.stack([w_bf, w_bb, w_af, w_ab, zeros, zeros, zeros, zeros], axis=2)
    w_g = w_g.reshape(d, GDN_HEADS * GATE_ROWS).T
    zh = jnp.zeros((GDN_HEADS,), F32)
    alog_rows = jnp.stack([zh, zh, A_log_f, A_log_b, zh, zh, zh, zh], axis=1).reshape(-1, 1)
    dtb_rows = jnp.stack([zh, zh, dt_bias_f, dt_bias_b, zh, zh, zh, zh], axis=1).reshape(-1, 1)
    cum_f, cum_b = _chunk_cumsum_mats(IN_SUB)
    cw = jnp.pad(conv_w, ((0, 8 - CONV_K), (0, 0)))
    cw = cw.reshape(8, 3 * GDN_HEADS, HD).transpose(1, 0, 2)
    row = lambda a: a.reshape(1, -1).astype(F32)
    return {
        "nw_pre": row(norm_mix_pre),
        "w_qkv": w_qkv.astype(BF16), "w_z": w_z.astype(BF16), "w_g": w_g.astype(BF16),
        "w_q": permute_heads(w_q, ATT_HEADS).astype(BF16),
        "w_k": permute_heads(w_k, ATT_KV_HEADS).astype(BF16),
        "w_v": w_v.T.astype(BF16),
        "alog_rows": alog_rows, "dtb_rows": dtb_rows, "cum_f": cum_f, "cum_b": cum_b,
        "rope": {T: _rope_tables(T) for T in seqs},
        "qnw": row(q_norm_w[perm]), "knw": row(k_norm_w[perm]),
        "conv_w": cw, "gdn_nw": row(gdn_norm_w),
        "w_out_a": w_out[:gw].astype(BF16), "w_out_b": w_out[gw:].astype(BF16),
        "nw_post": row(norm_mix_post), "nw_pre2": row(norm_mlp_pre),
        "w_up": w_up.astype(BF16), "w_down": w_down.astype(BF16), "nw_post2": row(norm_mlp_post),
    }


IN_TILE = 1024
MLP_TILE = 512
ATT_Q_TILE = 1024
ATT_K_TILE = 1024
FF_CHUNK = 1024


def _layer(x, P):
    B, T, d = x.shape
    x2 = x.reshape(B * T, d)
    qkv, z, gates, q, k, vt = _in_proj(x2, T, P, min(IN_TILE, T))
    o_a = _gdn(qkv.reshape(B, T, -1), z.reshape(B, T, -1), gates, P, B, T)
    o_b = _attn(q.reshape(B, T, -1), k.reshape(B, T, -1), vt, B, T, min(ATT_Q_TILE, T), min(ATT_K_TILE, T // 2))
    y = _out_mlp(x2, o_a.reshape(B * T, -1), o_b.reshape(B * T, -1), P, min(MLP_TILE, T), FF_CHUNK)
    return y.reshape(B, T, d)


def kernel(x_prompt, x_sample, norm_mix_pre, w_in, conv_w, A_log_f, A_log_b, dt_bias_f, dt_bias_b,
           gdn_norm_w, q_norm_w, k_norm_w, w_out, norm_mix_post, norm_mlp_pre, w_up, w_down,
           norm_mlp_post):
    y_prompt, y_sample = x_prompt, x_sample
    for l in range(w_in.shape[0]):
        P = _prepare(norm_mix_pre[l], w_in[l], conv_w[l], A_log_f[l], A_log_b[l], dt_bias_f[l],
                     dt_bias_b[l], gdn_norm_w[l], q_norm_w[l], k_norm_w[l], w_out[l], norm_mix_post[l],
                     norm_mlp_pre[l], w_up[l], w_down[l], norm_mlp_post[l],
                     seqs={y_prompt.shape[1], y_sample.shape[1]})
        y_prompt = _layer(y_prompt, P)
        y_sample = _layer(y_sample, P)
    return (y_prompt, y_sample)
```

```python
import functools
import math

import jax
import jax.numpy as jnp
import numpy as np
from jax import lax
from jax.experimental import pallas as pl
from jax.experimental.pallas import tpu as pltpu

F32 = jnp.float32
BF16 = jnp.bfloat16

EPS = 1e-6
GRID_W = 64
ROPE_THETA = 10000.0
CONV_K = 5
HD = 128
GDN_HEADS = 4
ATT_HEADS = 4
ATT_KV_HEADS = 2
ATT_GROUP = ATT_HEADS // ATT_KV_HEADS
LANES = 128
CHUNK = LANES
GATE_ROWS = 8
VMEM_LIMIT = 56 * 1024 * 1024
IN_SUB = 256
ATT_ONES_ROWS = 16

_NT = (((1,), (1,)), ((), ()))


def _dot(a, b):
    return jnp.dot(a, b, preferred_element_type=F32)


def _dot_nt(a, b):
    return lax.dot_general(a, b, _NT, preferred_element_type=F32)


def _rms(x, w):
    return x * lax.rsqrt(jnp.mean(x * x, axis=-1, keepdims=True) + EPS) * w


def _sigmoid(x):
    return 1.0 / (1.0 + jnp.exp(-x))


def _in_proj_kernel(x_ref, nw_ref, wqkv_ref, wz_ref, wg_ref, wq_ref, wk_ref, wv_ref,
                    alog_ref, dtb_ref, cumf_ref, cumb_ref, cos_ref, sin_ref, qnw_ref, knw_ref,
                    qkv_ref, z_ref, gate_ref, q_ref, k_ref, vt_ref, *, q_scale):
    for s in range(x_ref.shape[0] // IN_SUB):
        r = slice(s * IN_SUB, (s + 1) * IN_SUB)
        h = _rms(x_ref[r, :], nw_ref[...]).astype(BF16)

        qkv_ref[r, :] = _dot(h, wqkv_ref[...]).astype(BF16)
        z_ref[r, :] = _dot(h, wz_ref[...]).astype(BF16)
        vt_ref[:, r] = _dot_nt(wv_ref[...], h).astype(BF16)

        graw = _dot_nt(wg_ref[...], h)
        kind = lax.broadcasted_iota(jnp.int32, graw.shape, 0) % GATE_ROWS
        beta = _sigmoid(graw)
        a = graw + dtb_ref[...]
        softplus = jnp.maximum(a, 0.0) + jnp.log1p(jnp.exp(-jnp.abs(a)))
        g = -jnp.exp(alog_ref[...]) * softplus
        g = jnp.where(kind >= 2, g, 0.0)
        p1 = g.astype(BF16)
        r1 = g - p1.astype(F32)
        p2 = r1.astype(BF16)
        p3 = (r1 - p2.astype(F32)).astype(BF16)
        nrow = g.shape[0]
        parts = jnp.concatenate([p1, p2, p3], axis=0)
        cf = _dot(parts, cumf_ref[...])
        cb = _dot(parts, cumb_ref[...])
        cf = cf[:nrow] + cf[nrow:2 * nrow] + cf[2 * nrow:]
        cb = cb[:nrow] + cb[nrow:2 * nrow] + cb[2 * nrow:]
        gate_ref[:, r] = jnp.where(kind < 2, beta, jnp.where(kind == 2, cf, jnp.where(kind == 3, cb, 0.0)))

        cos = cos_ref[r, :]
        sin = sin_ref[r, :]

        def norm_rope(y, w, scale):
            y = _rms(y, w)
            y = y * cos + pltpu.roll(y, HD // 2, axis=1) * sin
            return (y * scale).astype(BF16) if scale != 1.0 else y.astype(BF16)

        qa = _dot(h, wq_ref[...])
        for hh in range(ATT_HEADS):
            q_ref[r, hh * HD:(hh + 1) * HD] = norm_rope(qa[:, hh * HD:(hh + 1) * HD], qnw_ref[...], q_scale)
        ka = _dot(h, wk_ref[...])
        for hh in range(ATT_KV_HEADS):
            k_ref[r, hh * HD:(hh + 1) * HD] = norm_rope(ka[:, hh * HD:(hh + 1) * HD], knw_ref[...], 1.0)


def _in_proj(x2, T, P, tm):
    n_tok, d = x2.shape
    nt = n_tok // tm
    tpb = T // tm
    const = lambda i: (0, 0)
    tile = lambda i: (i, 0)

    def full(a):
        return pl.BlockSpec(a.shape, const)

    in_specs = [
        pl.BlockSpec((tm, d), tile), full(P["nw_pre"]),
        full(P["w_qkv"]), full(P["w_z"]), full(P["w_g"]), full(P["w_q"]), full(P["w_k"]), full(P["w_v"]),
        full(P["alog_rows"]), full(P["dtb_rows"]), full(P["cum_f"]), full(P["cum_b"]),
        pl.BlockSpec((tm, HD), lambda i: (i % tpb, 0)), pl.BlockSpec((tm, HD), lambda i: (i % tpb, 0)),
        full(P["qnw"]), full(P["knw"]),
    ]
    n_gate = GDN_HEADS * GATE_ROWS
    out_shape = [
        jax.ShapeDtypeStruct((n_tok, 3 * GDN_HEADS * HD), BF16),
        jax.ShapeDtypeStruct((n_tok, GDN_HEADS * HD), BF16),
        jax.ShapeDtypeStruct((n_gate, n_tok), F32),
        jax.ShapeDtypeStruct((n_tok, ATT_HEADS * HD), BF16),
        jax.ShapeDtypeStruct((n_tok, ATT_KV_HEADS * HD), BF16),
        jax.ShapeDtypeStruct((ATT_KV_HEADS * HD, n_tok), BF16),
    ]
    out_specs = [
        pl.BlockSpec((tm, 3 * GDN_HEADS * HD), tile),
        pl.BlockSpec((tm, GDN_HEADS * HD), tile),
        pl.BlockSpec((n_gate, tm), lambda i: (0, i)),
        pl.BlockSpec((tm, ATT_HEADS * HD), tile),
        pl.BlockSpec((tm, ATT_KV_HEADS * HD), tile),
        pl.BlockSpec((ATT_KV_HEADS * HD, tm), lambda i: (0, i)),
    ]
    q_scale = (HD ** -0.5) * math.log2(math.e)
    cos, sin = P["rope"][T]
    return pl.pallas_call(
        functools.partial(_in_proj_kernel, q_scale=q_scale),
        grid=(nt,), in_specs=in_specs, out_specs=out_specs, out_shape=out_shape,
        compiler_params=pltpu.CompilerParams(dimension_semantics=("parallel",), vmem_limit_bytes=VMEM_LIMIT),
        name="in_proj",
    )(x2, P["nw_pre"], P["w_qkv"], P["w_z"], P["w_g"], P["w_q"], P["w_k"], P["w_v"],
      P["alog_rows"], P["dtb_rows"], P["cum_f"], P["cum_b"], cos, sin, P["qnw"], P["knw"])


PREP_CHUNKS = 4
GROUP_ROWS = PREP_CHUNKS * CHUNK
CONV_ROWS = GROUP_ROWS // 2
CONV_HALO = 16


def _conv_silu(x_ref, cw, win_ref, t0, seq):
    r, hl = CONV_ROWS, CONV_HALO
    prev = x_ref[pl.ds(pl.multiple_of(jnp.maximum(t0 - hl, 0), hl), hl), :].astype(F32)
    nxt = x_ref[pl.ds(pl.multiple_of(jnp.minimum(t0 + r, seq - hl), hl), hl), :].astype(F32)
    win_ref[0:hl, :] = jnp.where(t0 > 0, prev, 0.0)
    win_ref[hl:hl + r, :] = x_ref[pl.ds(t0, r), :].astype(F32)
    win_ref[hl + r:hl + r + hl, :] = jnp.where(t0 + r < seq, nxt, 0.0)
    first = hl - CONV_K // 2
    acc = cw[0:1, :] * win_ref[first:first + r, :]
    for j in range(1, CONV_K):
        acc = acc + cw[j:j + 1, :] * win_ref[first + j:first + j + r, :]
    return acc * _sigmoid(acc)


def _l2n(x, scale):
    return x * (lax.rsqrt(jnp.sum(x * x, axis=-1, keepdims=True) + EPS) * scale)


def _gdn_kernel(q_ref, k_ref, v_ref, z_ref, cwq_ref, cwk_ref, cwv_ref, gate_ref, nw_ref, o_ref,
                qn_ref, kn_ref, vn_ref, ob_ref, winq_ref, wink_ref, winv_ref,
                mqf_ref, nf_ref, decf_ref, mqb_ref, nb_ref, decb_ref, *, seq):
    c = CHUNK
    dir_refs = ((mqf_ref, nf_ref, decf_ref), (mqb_ref, nb_ref, decb_ref))

    cwq, cwk, cwv = cwq_ref[0], cwk_ref[0], cwv_ref[0]
    n_group = seq // GROUP_ROWS

    def conv_pieces(g):
        def piece(x_ref, cw, win_ref, dst_ref, normalise, scale, half):
            def run():
                t0 = pl.multiple_of(g * GROUP_ROWS + half * CONV_ROWS, CONV_ROWS)
                y = _conv_silu(x_ref, cw, win_ref, t0, seq)
                if normalise:
                    y = _l2n(y, scale)
                dst_ref[pl.ds(t0, CONV_ROWS), :] = y.astype(BF16)
            return run
        streams = ((q_ref, cwq, winq_ref, qn_ref, True, HD ** -0.5), (k_ref, cwk, wink_ref, kn_ref, True, 1.0),
                   (v_ref, cwv, winv_ref, vn_ref, False, 1.0))
        return [piece(*s, half) for s in streams for half in range(GROUP_ROWS // CONV_ROWS)]

    ri = lax.broadcasted_iota(jnp.int32, (c, c), 0)
    ci = lax.broadcasted_iota(jnp.int32, (c, c), 1)
    eye = ri == ci

    def col(row):
        return jnp.sum(jnp.where(eye, row, 0.0), axis=1, keepdims=True)

    def inv_unit_tri(mats, fillers):
        ts = [jnp.where(eye, 1.0, 0.0) - jnp.where((ri // 2) == (ci // 2), a, 0.0) for a in mats]
        fillers = list(fillers)
        per_level = -(-len(fillers) // (c.bit_length() - 2))
        b = 2
        while b < c:
            off = ((ri // (2 * b)) == (ci // (2 * b))) & ((ri // b) != (ci // b))
            tbs = [t.astype(BF16) for t in ts]
            xs = [_dot(jnp.where(off, a, 0.0).astype(BF16), tb).astype(BF16) for a, tb in zip(mats, tbs)]
            for f in fillers[:per_level]:
                f()
            fillers = fillers[per_level:]
            ts = [t - _dot(tb, x) for t, tb, x in zip(ts, tbs, xs)]
            b *= 2
        return ts

    def prep_group(probs, first, fillers):
        t0s = [pl.multiple_of(ch * c, c) for ch, d in probs]
        qs = [qn_ref[pl.ds(t0, c), :] for t0 in t0s]
        ks = [kn_ref[pl.ds(t0, c), :] for t0 in t0s]
        vs = [vn_ref[pl.ds(t0, c), :] for t0 in t0s]
        gts = [gate_ref[:, pl.ds(t0, c)] for t0 in t0s]
        kqk = [_dot_nt(jnp.concatenate([k, q], axis=0), k) for q, k in zip(qs, ks)]
        grams = [x[:c] for x in kqk]
        qks = [x[c:] for x in kqk]
        kts = [k.astype(F32).T for k in ks]
        beta_r = [g[d:d + 1] for g, (ch, d) in zip(gts, probs)]
        gc_r = [g[2 + d:3 + d] for g, (ch, d) in zip(gts, probs)]
        gc_c = [col(g) for g in gc_r]
        beta_c = [col(bb) for bb in beta_r]
        masks = [(ri <= ci) if d else (ri >= ci) for ch, d in probs]
        stricts = [(ri < ci) if d else (ri > ci) for ch, d in probs]
        es = [jnp.where(m, jnp.exp(jnp.where(m, gcc - gcr, 0.0)), 0.0) for m, gcc, gcr in zip(masks, gc_c, gc_r)]
        amats = [jnp.where(st, gram * e * bc, 0.0) for gram, st, e, bc in zip(grams, stricts, es, beta_c)]
        ts = inv_unit_tri(amats, fillers)
        us = [_dot((t * br).astype(BF16), v) for t, br, v in zip(ts, beta_r, vs)]
        ws = [_dot((t * (br * jnp.exp(gr))).astype(BF16), k) for t, br, gr, k in zip(ts, beta_r, gc_r, ks)]
        wus = [jnp.concatenate([w, u], axis=1).astype(BF16) for w, u in zip(ws, us)]
        g_last = [gr[:, 0:1] if d else gr[:, c - 1:c] for (ch, d), gr in zip(probs, gc_r)]
        kes = [(kt * jnp.exp(gl - gr)).astype(BF16) for kt, gl, gr in zip(kts, g_last, gc_r)]
        qkms = [(qk * e).astype(BF16) for qk, e in zip(qks, es)]
        mq = [_dot(jnp.concatenate([ke, qkm], axis=0), wu) for ke, qkm, wu in zip(kes, qkms, wus)]
        for i, (ch, d) in enumerate(probs):
            mn, qo = mq[i][:c], mq[i][c:]
            mq_ref, n_ref, dec_ref = dir_refs[d]
            m0 = pl.multiple_of(ch * (2 * c), 2 * c)
            mq_ref[pl.ds(m0, c), :] = mn[:, :HD].astype(BF16)
            q_dec = qs[i].astype(F32) * jnp.exp(gc_c[i])
            mq_ref[pl.ds(m0 + c, c), :] = (q_dec - qo[:, :HD]).astype(BF16)
            n_ref[pl.ds(t0s[i], c), :] = mn[:, HD:]
            dec_ref[pl.ds(pl.multiple_of(ch * 8, 8), 8), :] = jnp.broadcast_to(jnp.exp(g_last[i]), (8, HD))
            if first:
                ob_ref[pl.ds(t0s[i], c), :] = qo[:, HD:]
            else:
                ob_ref[pl.ds(t0s[i], c), :] += qo[:, HD:]

    def step(chunk, s, refs):
        mq_ref, n_ref, dec_ref = refs
        t0 = pl.multiple_of(chunk * c, c)
        m0 = pl.multiple_of(chunk * (2 * c), 2 * c)
        s_bf = s.astype(BF16)
        ms = _dot(mq_ref[pl.ds(m0, c), :], s_bf)
        dec = dec_ref[pl.ds(pl.multiple_of(chunk * 8, 8), 8), :][0:1]
        ob_ref[pl.ds(t0, c), :] += _dot(mq_ref[pl.ds(m0 + c, c), :], s_bf)
        return s * dec - ms + n_ref[pl.ds(t0, c), :]

    half = n_group // 2

    def group_chunks(g, reverse=False):
        chunks = [g * PREP_CHUNKS + cc for cc in range(PREP_CHUNKS)]
        return chunks[::-1] if reverse else chunks

    def scan_steps(g, state):
        def one(d, chunk):
            def run():
                state[d] = step(chunk, state[d], dir_refs[d])
            return run
        pairs = zip(group_chunks(g), group_chunks(n_group - 1 - g, reverse=True))
        return [f for cf, cb in pairs for f in (one(0, cf), one(1, cb))]

    def next_convs(g):
        return conv_pieces(jnp.minimum(g + 1, half - 1)) + conv_pieces(jnp.maximum(n_group - 2 - g, half))

    def iteration(g, first, fillers):
        probs = [(ch, 0) for ch in group_chunks(g)] + [(ch, 1) for ch in group_chunks(n_group - 1 - g)]
        prep_group(probs, first, fillers)

    for run in conv_pieces(0) + conv_pieces(n_group - 1):
        run()
    iteration(0, True, next_convs(0))

    def first_half(g, carry):
        state = list(carry)
        iteration(g, True, scan_steps(g - 1, state) + next_convs(g))
        return tuple(state)

    nw = nw_ref[...]

    def norm_pieces(g):
        def piece(half_idx):
            def run():
                t0 = pl.multiple_of(g * GROUP_ROWS + half_idx * CONV_ROWS, CONV_ROWS)
                o = _rms(ob_ref[pl.ds(t0, CONV_ROWS), :], nw)
                zz = z_ref[pl.ds(t0, CONV_ROWS), :].astype(F32)
                o_ref[pl.ds(t0, CONV_ROWS), :] = (o * (zz * _sigmoid(zz))).astype(BF16)
            return run
        return [piece(i) for i in range(GROUP_ROWS // CONV_ROWS)]

    def second_half(g, carry):
        state = list(carry)
        iteration(g, False, scan_steps(g - 1, state))
        return tuple(state)

    def second_half_norm(g, carry):
        state = list(carry)
        iteration(g, False, scan_steps(g - 1, state) + norm_pieces(g - 2) + norm_pieces(n_group + 1 - g))
        return tuple(state)

    s0 = jnp.zeros((HD, HD), F32)
    norm_from = min(half + 2, n_group)
    carry = lax.fori_loop(1, half, first_half, (s0, s0))
    carry = lax.fori_loop(half, norm_from, second_half, carry)
    carry = lax.fori_loop(norm_from, n_group, second_half_norm, carry)
    state = list(carry)
    for run in scan_steps(n_group - 1, state):
        run()
    normed = {grp for g in range(norm_from, n_group) for grp in (g - 2, n_group + 1 - g)}
    for g in sorted(set(range(n_group)) - normed):
        for run in norm_pieces(g):
            run()


def _gdn(qkv, z, gates, P, B, T):
    nh = GDN_HEADS
    in_specs = [
        pl.BlockSpec((None, T, HD), lambda b, h: (b, 0, h)),
        pl.BlockSpec((None, T, HD), lambda b, h: (b, 0, nh + h)),
        pl.BlockSpec((None, T, HD), lambda b, h: (b, 0, 2 * nh + h)),
        pl.BlockSpec((None, T, HD), lambda b, h: (b, 0, h)),
        pl.BlockSpec((1, 8, HD), lambda b, h: (h, 0, 0)),
        pl.BlockSpec((1, 8, HD), lambda b, h: (nh + h, 0, 0)),
        pl.BlockSpec((1, 8, HD), lambda b, h: (2 * nh + h, 0, 0)),
        pl.BlockSpec((GATE_ROWS, T), lambda b, h: (h, b)),
        pl.BlockSpec((1, HD), lambda b, h: (0, 0)),
    ]
    scratch = [
        pltpu.VMEM((T, HD), BF16), pltpu.VMEM((T, HD), BF16), pltpu.VMEM((T, HD), BF16),
        pltpu.VMEM((T, HD), F32),
        pltpu.VMEM((CONV_ROWS + 2 * CONV_HALO, HD), F32),
        pltpu.VMEM((CONV_ROWS + 2 * CONV_HALO, HD), F32),
        pltpu.VMEM((CONV_ROWS + 2 * CONV_HALO, HD), F32),
    ]
    per_direction = [pltpu.VMEM((2 * T, HD), BF16), pltpu.VMEM((T, HD), F32),
                     pltpu.VMEM((T // CHUNK * 8, HD), F32)]
    scratch = scratch + per_direction + per_direction
    return pl.pallas_call(
        functools.partial(_gdn_kernel, seq=T),
        grid=(B, nh), in_specs=in_specs,
        out_specs=pl.BlockSpec((None, T, HD), lambda b, h: (b, 0, h)),
        out_shape=jax.ShapeDtypeStruct((B, T, nh * HD), BF16),
        scratch_shapes=scratch,
        compiler_params=pltpu.CompilerParams(dimension_semantics=("parallel", "parallel"),
                                             vmem_limit_bytes=VMEM_LIMIT),
        name="gdn",
    )(qkv, qkv, qkv, z, P["conv_w"], P["conv_w"], P["conv_w"], gates, P["gdn_nw"])


def _attn_kernel(q_ref, k_ref, vt_ref, o_ref, m_ref, acc_ref, sta_ref, stb_ref, *, seq, tk):
    tq = q_ref.shape[0]
    q = jnp.concatenate([q_ref[:, g * HD:(g + 1) * HD] for g in range(ATT_GROUP)], axis=0)
    m_ref[...] = jnp.full_like(m_ref, -jnp.inf)
    acc_ref[...] = jnp.zeros_like(acc_ref)
    ones = jnp.ones((ATT_ONES_ROWS, tk), BF16)
    n_kv = seq // tk

    def row0(j):
        return j * tk if isinstance(j, int) else pl.multiple_of(j * tk, tk)

    def scores(j, st_ref):
        st_ref[...] = _dot_nt(k_ref[pl.ds(row0(j), tk), :], q)

    def consume(j, st_ref):
        t0 = row0(j)
        m_old = m_ref[...]
        m_new = jnp.maximum(m_old, jnp.max(st_ref[...], axis=0, keepdims=True))
        pt = jnp.exp2(st_ref[...] - m_new).astype(BF16)
        alpha = jnp.exp2(m_old - m_new)
        vt = jnp.concatenate([vt_ref[:, pl.ds(t0, tk)], ones], axis=0)
        acc_ref[...] = alpha * acc_ref[...] + _dot(vt, pt)
        m_ref[...] = m_new

    scores(0, sta_ref)

    def body(i, carry):
        scores(2 * i + 1, stb_ref)
        consume(2 * i, sta_ref)
        scores(2 * i + 2, sta_ref)
        consume(2 * i + 1, stb_ref)
        return carry

    lax.fori_loop(0, n_kv // 2 - 1, body, 0)
    scores(n_kv - 1, stb_ref)
    consume(n_kv - 2, sta_ref)
    consume(n_kv - 1, stb_ref)
    acc = acc_ref[...]
    out = (acc[:HD] / acc[HD:HD + 1]).T
    for g in range(ATT_GROUP):
        o_ref[:, g * HD:(g + 1) * HD] = out[g * tq:(g + 1) * tq].astype(BF16)


def _attn(q, k, vt, B, T, tq, tk):
    gw = ATT_GROUP * HD
    return pl.pallas_call(
        functools.partial(_attn_kernel, seq=T, tk=tk),
        grid=(B, ATT_KV_HEADS, T // tq),
        in_specs=[
            pl.BlockSpec((None, tq, gw), lambda b, h, i: (b, i, h)),
            pl.BlockSpec((None, T, HD), lambda b, h, i: (b, 0, h)),
            pl.BlockSpec((HD, T), lambda b, h, i: (h, b)),
        ],
        out_specs=pl.BlockSpec((None, tq, gw), lambda b, h, i: (b, i, h)),
        out_shape=jax.ShapeDtypeStruct((B, T, ATT_HEADS * HD), BF16),
        scratch_shapes=[pltpu.VMEM((1, ATT_GROUP * tq), F32),
                        pltpu.VMEM((HD + ATT_ONES_ROWS, ATT_GROUP * tq), F32),
                        pltpu.VMEM((tk, ATT_GROUP * tq), F32), pltpu.VMEM((tk, ATT_GROUP * tq), F32)],
        compiler_params=pltpu.CompilerParams(dimension_semantics=("parallel", "parallel", "parallel"),
                                             vmem_limit_bytes=VMEM_LIMIT),
        name="attn",
    )(q, k, vt)


def _out_mlp_kernel(x_ref, oa_ref, ob_ref, woa_ref, wob_ref, npost_ref, npre2_ref, wup_ref, wdn_ref,
                    npost2_ref, y_ref, *, ff_chunk):
    mix = _dot(oa_ref[...], woa_ref[...]) + _dot(ob_ref[...], wob_ref[...])
    x1 = x_ref[...] + _rms(mix, npost_ref[...])
    hm = _rms(x1, npre2_ref[...]).astype(BF16)
    d_ff = wup_ref.shape[1]
    f = jnp.zeros(x1.shape, F32)
    for cc in range(d_ff // ff_chunk):
        hc = jnp.maximum(_dot(hm, wup_ref[:, cc * ff_chunk:(cc + 1) * ff_chunk]), 0.0)
        f = f + _dot((hc * hc).astype(BF16), wdn_ref[cc * ff_chunk:(cc + 1) * ff_chunk, :])
    y_ref[...] = x1 + _rms(f, npost2_ref[...])


def _out_mlp(x2, oa, ob, P, tm, ff_chunk):
    n_tok, d = x2.shape
    const = lambda i: (0, 0)
    tile = lambda i: (i, 0)
    single = pl.Buffered(1)

    def full(a):
        return pl.BlockSpec(a.shape, const, pipeline_mode=single)

    return pl.pallas_call(
        functools.partial(_out_mlp_kernel, ff_chunk=ff_chunk),
        grid=(n_tok // tm,),
        in_specs=[pl.BlockSpec((tm, d), tile), pl.BlockSpec((tm, oa.shape[1]), tile),
                  pl.BlockSpec((tm, ob.shape[1]), tile),
                  full(P["w_out_a"]), full(P["w_out_b"]), full(P["nw_post"]), full(P["nw_pre2"]),
                  full(P["w_up"]), full(P["w_down"]), full(P["nw_post2"])],
        out_specs=pl.BlockSpec((tm, d), tile),
        out_shape=jax.ShapeDtypeStruct((n_tok, d), F32),
        compiler_params=pltpu.CompilerParams(dimension_semantics=("parallel",), vmem_limit_bytes=VMEM_LIMIT),
        name="out_mlp",
    )(x2, oa, ob, P["w_out_a"], P["w_out_b"], P["nw_post"], P["nw_pre2"], P["w_up"], P["w_down"],
      P["nw_post2"])


def _rope_tables(T):
    n_freq = HD // 4
    t = np.arange(T)
    inv_freq = ROPE_THETA ** (-np.arange(n_freq, dtype=np.float32) / n_freq)
    row_ang = jnp.asarray((t // GRID_W).astype(np.float32))[:, None] * jnp.asarray(inv_freq)[None, :]
    col_ang = jnp.asarray((t % GRID_W).astype(np.float32))[:, None] * jnp.asarray(inv_freq)[None, :]
    cr, sr, cc, sc = jnp.cos(row_ang), jnp.sin(row_ang), jnp.cos(col_ang), jnp.sin(col_ang)
    return jnp.concatenate([cr, cc, cr, cc], axis=1), jnp.concatenate([-sr, -sc, sr, sc], axis=1)


def _chunk_cumsum_mats(tm):
    t = np.arange(tm)
    same = (t[:, None] // CHUNK) == (t[None, :] // CHUNK)
    fwd = same & (t[:, None] <= t[None, :])
    bwd = same & (t[:, None] >= t[None, :])
    return jnp.asarray(fwd, BF16), jnp.asarray(bwd, BF16)


def _prepare(norm_mix_pre, w_in, conv_w, A_log_f, A_log_b, dt_bias_f, dt_bias_b, gdn_norm_w, q_norm_w,
             k_norm_w, w_out, norm_mix_post, norm_mlp_pre, w_up, w_down, norm_mlp_post, seqs):
    gw = GDN_HEADS * HD
    sizes = (3 * gw, gw, GDN_HEADS, GDN_HEADS, GDN_HEADS, GDN_HEADS, ATT_HEADS * HD, ATT_KV_HEADS * HD,
             ATT_KV_HEADS * HD)
    offs = np.concatenate([[0], np.cumsum(sizes)])
    cols = [w_in[:, offs[i]:offs[i + 1]] for i in range(len(sizes))]
    w_qkv, w_z, w_bf, w_bb, w_af, w_ab, w_q, w_k, w_v = cols
    d = w_in.shape[0]
    perm = np.concatenate([np.arange(0, 32), np.arange(64, 96), np.arange(32, 64), np.arange(96, 128)])

    def permute_heads(w, n):
        return w.reshape(d, n, HD)[:, :, perm].reshape(d, n * HD)

    zeros = jnp.zeros((d, GDN_HEADS), F32)
    w_g = jnp.stack([w_bf, w_bb, w_af, w_ab, zeros, zeros, zeros, zeros], axis=2)
    w_g = w_g.reshape(d, GDN_HEADS * GATE_ROWS).T
    zh = jnp.zeros((GDN_HEADS,), F32)
    alog_rows = jnp.stack([zh, zh, A_log_f, A_log_b, zh, zh, zh, zh], axis=1).reshape(-1, 1)
    dtb_rows = jnp.stack([zh, zh, dt_bias_f, dt_bias_b, zh, zh, zh, zh], axis=1).reshape(-1, 1)
    cum_f, cum_b = _chunk_cumsum_mats(IN_SUB)
    cw = jnp.pad(conv_w, ((0, 8 - CONV_K), (0, 0)))
    cw = cw.reshape(8, 3 * GDN_HEADS, HD).transpose(1, 0, 2)
    row = lambda a: a.reshape(1, -1).astype(F32)
    return {
        "nw_pre": row(norm_mix_pre),
        "w_qkv": w_qkv.astype(BF16), "w_z": w_z.astype(BF16), "w_g": w_g.astype(BF16),
        "w_q": permute_heads(w_q, ATT_HEADS).astype(BF16),
        "w_k": permute_heads(w_k, ATT_KV_HEADS).astype(BF16),
        "w_v": w_v.T.astype(BF16),
        "alog_rows": alog_rows, "dtb_rows": dtb_rows, "cum_f": cum_f, "cum_b": cum_b,
        "rope": {T: _rope_tables(T) for T in seqs},
        "qnw": row(q_norm_w[perm]), "knw": row(k_norm_w[perm]),
        "conv_w": cw, "gdn_nw": row(gdn_norm_w),
        "w_out_a": w_out[:gw].astype(BF16), "w_out_b": w_out[gw:].astype(BF16),
        "nw_post": row(norm_mix_post), "nw_pre2": row(norm_mlp_pre),
        "w_up": w_up.astype(BF16), "w_down": w_down.astype(BF16), "nw_post2": row(norm_mlp_post),
    }


IN_TILE = 1024
MLP_TILE = 512
ATT_Q_TILE = 1024
ATT_K_TILE = 1024
FF_CHUNK = 1024


def _layer(x, P):
    B, T, d = x.shape
    x2 = x.reshape(B * T, d)
    qkv, z, gates, q, k, vt = _in_proj(x2, T, P, min(IN_TILE, T))
    o_a = _gdn(qkv.reshape(B, T, -1), z.reshape(B, T, -1), gates, P, B, T)
    o_b = _attn(q.reshape(B, T, -1), k.reshape(B, T, -1), vt, B, T, min(ATT_Q_TILE, T), min(ATT_K_TILE, T // 2))
    y = _out_mlp(x2, o_a.reshape(B * T, -1), o_b.reshape(B * T, -1), P, min(MLP_TILE, T), FF_CHUNK)
    return y.reshape(B, T, d)


def kernel(x_prompt, x_sample, norm_mix_pre, w_in, conv_w, A_log_f, A_log_b, dt_bias_f, dt_bias_b,
           gdn_norm_w, q_norm_w, k_norm_w, w_out, norm_mix_post, norm_mlp_pre, w_up, w_down,
           norm_mlp_post):
    y_prompt, y_sample = x_prompt, x_sample
    for l in range(w_in.shape[0]):
        P = _prepare(norm_mix_pre[l], w_in[l], conv_w[l], A_log_f[l], A_log_b[l], dt_bias_f[l],
                     dt_bias_b[l], gdn_norm_w[l], q_norm_w[l], k_norm_w[l], w_out[l], norm_mix_post[l],
                     norm_mlp_pre[l], w_up[l], w_down[l], norm_mlp_post[l],
                     seqs={y_prompt.shape[1], y_sample.shape[1]})
        y_prompt = _layer(y_prompt, P)
        y_sample = _layer(y_sample, P)
    return (y_prompt, y_sample)
```

```python
import functools
import math

import jax
import jax.numpy as jnp
import numpy as np
from jax import lax
from jax.experimental import pallas as pl
from jax.experimental.pallas import tpu as pltpu

F32 = jnp.float32
BF16 = jnp.bfloat16

EPS = 1e-6
GRID_W = 64
ROPE_THETA = 10000.0
CONV_K = 5
HD = 128
GDN_HEADS = 4
ATT_HEADS = 4
ATT_KV_HEADS = 2
ATT_GROUP = ATT_HEADS // ATT_KV_HEADS
LANES = 128
CHUNK = LANES
GATE_ROWS = 8
VMEM_LIMIT = 56 * 1024 * 1024
IN_SUB = 256
ATT_ONES_ROWS = 16

_NT = (((1,), (1,)), ((), ()))


def _dot(a, b):
    return jnp.dot(a, b, preferred_element_type=F32)


def _dot_nt(a, b):
    return lax.dot_general(a, b, _NT, preferred_element_type=F32)


def _rms(x, w):
    return x * lax.rsqrt(jnp.mean(x * x, axis=-1, keepdims=True) + EPS) * w


def _sigmoid(x):
    return 1.0 / (1.0 + jnp.exp(-x))


def _in_proj_kernel(x_ref, nw_ref, wqkv_ref, wz_ref, wg_ref, wq_ref, wk_ref, wv_ref,
                    alog_ref, dtb_ref, cumf_ref, cumb_ref, cos_ref, sin_ref, qnw_ref, knw_ref,
                    qkv_ref, z_ref, gate_ref, q_ref, k_ref, vt_ref, *, q_scale):
    for s in range(x_ref.shape[0] // IN_SUB):
        r = slice(s * IN_SUB, (s + 1) * IN_SUB)
        h = _rms(x_ref[r, :], nw_ref[...]).astype(BF16)

        qkv_ref[r, :] = _dot(h, wqkv_ref[...]).astype(BF16)
        z_ref[r, :] = _dot(h, wz_ref[...]).astype(BF16)
        vt_ref[:, r] = _dot_nt(wv_ref[...], h).astype(BF16)

        graw = _dot_nt(wg_ref[...], h)
        kind = lax.broadcasted_iota(jnp.int32, graw.shape, 0) % GATE_ROWS
        beta = _sigmoid(graw)
        a = graw + dtb_ref[...]
        softplus = jnp.maximum(a, 0.0) + jnp.log1p(jnp.exp(-jnp.abs(a)))
        g = -jnp.exp(alog_ref[...]) * softplus
        g = jnp.where(kind >= 2, g, 0.0)
        p1 = g.astype(BF16)
        r1 = g - p1.astype(F32)
        p2 = r1.astype(BF16)
        p3 = (r1 - p2.astype(F32)).astype(BF16)
        nrow = g.shape[0]
        parts = jnp.concatenate([p1, p2, p3], axis=0)
        cf = _dot(parts, cumf_ref[...])
        cb = _dot(parts, cumb_ref[...])
        cf = cf[:nrow] + cf[nrow:2 * nrow] + cf[2 * nrow:]
        cb = cb[:nrow] + cb[nrow:2 * nrow] + cb[2 * nrow:]
        gate_ref[:, r] = jnp.where(kind < 2, beta, jnp.where(kind == 2, cf, jnp.where(kind == 3, cb, 0.0)))

        cos = cos_ref[r, :]
        sin = sin_ref[r, :]

        def norm_rope(y, w, scale):
            y = _rms(y, w)
            y = y * cos + pltpu.roll(y, HD // 2, axis=1) * sin
            return (y * scale).astype(BF16) if scale != 1.0 else y.astype(BF16)

        qa = _dot(h, wq_ref[...])
        for hh in range(ATT_HEADS):
            q_ref[r, hh * HD:(hh + 1) * HD] = norm_rope(qa[:, hh * HD:(hh + 1) * HD], qnw_ref[...], q_scale)
        ka = _dot(h, wk_ref[...])
        for hh in range(ATT_KV_HEADS):
            k_ref[r, hh * HD:(hh + 1) * HD] = norm_rope(ka[:, hh * HD:(hh + 1) * HD], knw_ref[...], 1.0)


def _in_proj(x2, T, P, tm):
    n_tok, d = x2.shape
    nt = n_tok // tm
    tpb = T // tm
    const = lambda i: (0, 0)
    tile = lambda i: (i, 0)

    def full(a):
        return pl.BlockSpec(a.shape, const)

    in_specs = [
        pl.BlockSpec((tm, d), tile), full(P["nw_pre"]),
        full(P["w_qkv"]), full(P["w_z"]), full(P["w_g"]), full(P["w_q"]), full(P["w_k"]), full(P["w_v"]),
        full(P["alog_rows"]), full(P["dtb_rows"]), full(P["cum_f"]), full(P["cum_b"]),
        pl.BlockSpec((tm, HD), lambda i: (i % tpb, 0)), pl.BlockSpec((tm, HD), lambda i: (i % tpb, 0)),
        full(P["qnw"]), full(P["knw"]),
    ]
    n_gate = GDN_HEADS * GATE_ROWS
    out_shape = [
        jax.ShapeDtypeStruct((n_tok, 3 * GDN_HEADS * HD), BF16),
        jax.ShapeDtypeStruct((n_tok, GDN_HEADS * HD), BF16),
        jax.ShapeDtypeStruct((n_gate, n_tok), F32),
        jax.ShapeDtypeStruct((n_tok, ATT_HEADS * HD), BF16),
        jax.ShapeDtypeStruct((n_tok, ATT_KV_HEADS * HD), BF16),
        jax.ShapeDtypeStruct((ATT_KV_HEADS * HD, n_tok), BF16),
    ]
    out_specs = [
        pl.BlockSpec((tm, 3 * GDN_HEADS * HD), tile),
        pl.BlockSpec((tm, GDN_HEADS * HD), tile),
        pl.BlockSpec((n_gate, tm), lambda i: (0, i)),
        pl.BlockSpec((tm, ATT_HEADS * HD), tile),
        pl.BlockSpec((tm, ATT_KV_HEADS * HD), tile),
        pl.BlockSpec((ATT_KV_HEADS * HD, tm), lambda i: (0, i)),
    ]
    q_scale = (HD ** -0.5) * math.log2(math.e)
    cos, sin = P["rope"][T]
    return pl.pallas_call(
        functools.partial(_in_proj_kernel, q_scale=q_scale),
        grid=(nt,), in_specs=in_specs, out_specs=out_specs, out_shape=out_shape,
        compiler_params=pltpu.CompilerParams(dimension_semantics=("parallel",), vmem_limit_bytes=VMEM_LIMIT),
        name="in_proj",
    )(x2, P["nw_pre"], P["w_qkv"], P["w_z"], P["w_g"], P["w_q"], P["w_k"], P["w_v"],
      P["alog_rows"], P["dtb_rows"], P["cum_f"], P["cum_b"], cos, sin, P["qnw"], P["knw"])


PREP_CHUNKS = 4
PREP_CHUNKS_SHORT = 8
SHORT_SEQ = 2048
CONV_ROWS = 256
CONV_HALO = 16


def _conv_silu(x_ref, cw, win_ref, t0, seq):
    r, hl = CONV_ROWS, CONV_HALO
    prev = x_ref[pl.ds(pl.multiple_of(jnp.maximum(t0 - hl, 0), hl), hl), :].astype(F32)
    nxt = x_ref[pl.ds(pl.multiple_of(jnp.minimum(t0 + r, seq - hl), hl), hl), :].astype(F32)
    win_ref[0:hl, :] = jnp.where(t0 > 0, prev, 0.0)
    win_ref[hl:hl + r, :] = x_ref[pl.ds(t0, r), :].astype(F32)
    win_ref[hl + r:hl + r + hl, :] = jnp.where(t0 + r < seq, nxt, 0.0)
    first = hl - CONV_K // 2
    acc = cw[0:1, :] * win_ref[first:first + r, :]
    for j in range(1, CONV_K):
        acc = acc + cw[j:j + 1, :] * win_ref[first + j:first + j + r, :]
    return acc * _sigmoid(acc)


def _l2n(x, scale):
    return x * (lax.rsqrt(jnp.sum(x * x, axis=-1, keepdims=True) + EPS) * scale)


def _gdn_kernel(q_ref, k_ref, v_ref, z_ref, cwq_ref, cwk_ref, cwv_ref, gate_ref, nw_ref, o_ref,
                qn_ref, kn_ref, vn_ref, ob_ref, winq_ref, wink_ref, winv_ref,
                mqf_ref, nf_ref, decf_ref, mqb_ref, nb_ref, decb_ref, *, seq, prep_chunks):
    c = CHUNK
    group_rows = prep_chunks * c
    dir_refs = ((mqf_ref, nf_ref, decf_ref), (mqb_ref, nb_ref, decb_ref))

    cwq, cwk, cwv = cwq_ref[0], cwk_ref[0], cwv_ref[0]
    n_group = seq // group_rows

    def conv_pieces(g):
        def piece(x_ref, cw, win_ref, dst_ref, normalise, scale, half):
            def run():
                t0 = pl.multiple_of(g * group_rows + half * CONV_ROWS, CONV_ROWS)
                y = _conv_silu(x_ref, cw, win_ref, t0, seq)
                if normalise:
                    y = _l2n(y, scale)
                dst_ref[pl.ds(t0, CONV_ROWS), :] = y.astype(BF16)
            return run
        streams = ((q_ref, cwq, winq_ref, qn_ref, True, HD ** -0.5), (k_ref, cwk, wink_ref, kn_ref, True, 1.0),
                   (v_ref, cwv, winv_ref, vn_ref, False, 1.0))
        return [piece(*s, half) for s in streams for half in range(group_rows // CONV_ROWS)]

    ri = lax.broadcasted_iota(jnp.int32, (c, c), 0)
    ci = lax.broadcasted_iota(jnp.int32, (c, c), 1)
    eye = ri == ci

    def col(row):
        return jnp.sum(jnp.where(eye, row, 0.0), axis=1, keepdims=True)

    def inv_unit_tri(mats, fillers):
        ts = [jnp.where(eye, 1.0, 0.0) - jnp.where((ri // 2) == (ci // 2), a, 0.0) for a in mats]
        fillers = list(fillers)
        per_level = -(-len(fillers) // (c.bit_length() - 2))
        b = 2
        while b < c:
            off = ((ri // (2 * b)) == (ci // (2 * b))) & ((ri // b) != (ci // b))
            tbs = [t.astype(BF16) for t in ts]
            xs = [_dot(jnp.where(off, a, 0.0).astype(BF16), tb).astype(BF16) for a, tb in zip(mats, tbs)]
            for f in fillers[:per_level]:
                f()
            fillers = fillers[per_level:]
            ts = [t - _dot(tb, x) for t, tb, x in zip(ts, tbs, xs)]
            b *= 2
        return ts

    def prep_group(probs, first, fillers):
        t0s = [pl.multiple_of(ch * c, c) for ch, d in probs]
        qs = [qn_ref[pl.ds(t0, c), :] for t0 in t0s]
        ks = [kn_ref[pl.ds(t0, c), :] for t0 in t0s]
        vs = [vn_ref[pl.ds(t0, c), :] for t0 in t0s]
        gts = [gate_ref[:, pl.ds(t0, c)] for t0 in t0s]
        kqk = [_dot_nt(jnp.concatenate([k, q], axis=0), k) for q, k in zip(qs, ks)]
        grams = [x[:c] for x in kqk]
        qks = [x[c:] for x in kqk]
        kts = [k.astype(F32).T for k in ks]
        beta_r = [g[d:d + 1] for g, (ch, d) in zip(gts, probs)]
        gc_r = [g[2 + d:3 + d] for g, (ch, d) in zip(gts, probs)]
        gc_c = [col(g) for g in gc_r]
        beta_c = [col(bb) for bb in beta_r]
        masks = [(ri <= ci) if d else (ri >= ci) for ch, d in probs]
        stricts = [(ri < ci) if d else (ri > ci) for ch, d in probs]
        es = [jnp.where(m, jnp.exp(jnp.where(m, gcc - gcr, 0.0)), 0.0) for m, gcc, gcr in zip(masks, gc_c, gc_r)]
        amats = [jnp.where(st, gram * e * bc, 0.0) for gram, st, e, bc in zip(grams, stricts, es, beta_c)]
        ts = inv_unit_tri(amats, fillers)
        us = [_dot((t * br).astype(BF16), v) for t, br, v in zip(ts, beta_r, vs)]
        ws = [_dot((t * (br * jnp.exp(gr))).astype(BF16), k) for t, br, gr, k in zip(ts, beta_r, gc_r, ks)]
        wus = [jnp.concatenate([w, u], axis=1).astype(BF16) for w, u in zip(ws, us)]
        g_last = [gr[:, 0:1] if d else gr[:, c - 1:c] for (ch, d), gr in zip(probs, gc_r)]
        kes = [(kt * jnp.exp(gl - gr)).astype(BF16) for kt, gl, gr in zip(kts, g_last, gc_r)]
        qkms = [(qk * e).astype(BF16) for qk, e in zip(qks, es)]
        mq = [_dot(jnp.concatenate([ke, qkm], axis=0), wu) for ke, qkm, wu in zip(kes, qkms, wus)]
        for i, (ch, d) in enumerate(probs):
            mn, qo = mq[i][:c], mq[i][c:]
            mq_ref, n_ref, dec_ref = dir_refs[d]
            m0 = pl.multiple_of(ch * (2 * c), 2 * c)
            mq_ref[pl.ds(m0, c), :] = mn[:, :HD].astype(BF16)
            q_dec = qs[i].astype(F32) * jnp.exp(gc_c[i])
            mq_ref[pl.ds(m0 + c, c), :] = (q_dec - qo[:, :HD]).astype(BF16)
            n_ref[pl.ds(t0s[i], c), :] = mn[:, HD:]
            dec_ref[pl.ds(pl.multiple_of(ch * 8, 8), 8), :] = jnp.broadcast_to(jnp.exp(g_last[i]), (8, HD))
            if first:
                ob_ref[pl.ds(t0s[i], c), :] = qo[:, HD:]
            else:
                ob_ref[pl.ds(t0s[i], c), :] += qo[:, HD:]

    def step(chunk, s, refs):
        mq_ref, n_ref, dec_ref = refs
        t0 = pl.multiple_of(chunk * c, c)
        m0 = pl.multiple_of(chunk * (2 * c), 2 * c)
        s_bf = s.astype(BF16)
        ms = _dot(mq_ref[pl.ds(m0, c), :], s_bf)
        dec = dec_ref[pl.ds(pl.multiple_of(chunk * 8, 8), 8), :][0:1]
        ob_ref[pl.ds(t0, c), :] += _dot(mq_ref[pl.ds(m0 + c, c), :], s_bf)
        return s * dec - ms + n_ref[pl.ds(t0, c), :]

    half = n_group // 2

    def group_chunks(g, reverse=False):
        chunks = [g * prep_chunks + cc for cc in range(prep_chunks)]
        return chunks[::-1] if reverse else chunks

    def scan_steps(g, state):
        def one(d, chunk):
            def run():
                state[d] = step(chunk, state[d], dir_refs[d])
            return run
        pairs = zip(group_chunks(g), group_chunks(n_group - 1 - g, reverse=True))
        return [f for cf, cb in pairs for f in (one(0, cf), one(1, cb))]

    def next_convs(g):
        return conv_pieces(g + 1) + conv_pieces(n_group - 2 - g)

    def iteration(g, first, fillers):
        probs = [(ch, 0) for ch in group_chunks(g)] + [(ch, 1) for ch in group_chunks(n_group - 1 - g)]
        prep_group(probs, first, fillers)

    for run in conv_pieces(0) + conv_pieces(n_group - 1):
        run()
    iteration(0, True, next_convs(0))

    def first_half(g, carry):
        state = list(carry)
        iteration(g, True, scan_steps(g - 1, state) + next_convs(g))
        return tuple(state)

    nw = nw_ref[...]

    def norm_pieces(g):
        def piece(half_idx):
            def run():
                t0 = pl.multiple_of(g * group_rows + half_idx * CONV_ROWS, CONV_ROWS)
                o = _rms(ob_ref[pl.ds(t0, CONV_ROWS), :], nw)
                zz = z_ref[pl.ds(t0, CONV_ROWS), :].astype(F32)
                o_ref[pl.ds(t0, CONV_ROWS), :] = (o * (zz * _sigmoid(zz))).astype(BF16)
            return run
        return [piece(i) for i in range(group_rows // CONV_ROWS)]

    def second_half(g, carry):
        state = list(carry)
        iteration(g, False, scan_steps(g - 1, state))
        return tuple(state)

    def second_half_norm(g, carry):
        state = list(carry)
        iteration(g, False, scan_steps(g - 1, state) + norm_pieces(g - 2) + norm_pieces(n_group + 1 - g))
        return tuple(state)

    s0 = jnp.zeros((HD, HD), F32)
    norm_from = min(half + 2, n_group)
    carry = lax.fori_loop(1, half, first_half, (s0, s0))
    carry = lax.fori_loop(half, norm_from, second_half, carry)
    carry = lax.fori_loop(norm_from, n_group, second_half_norm, carry)
    state = list(carry)
    for run in scan_steps(n_group - 1, state):
        run()
    normed = {grp for g in range(norm_from, n_group) for grp in (g - 2, n_group + 1 - g)}
    for g in sorted(set(range(n_group)) - normed):
        for run in norm_pieces(g):
            run()


def _gdn(qkv, z, gates, P, B, T):
    nh = GDN_HEADS
    in_specs = [
        pl.BlockSpec((None, T, HD), lambda b, h: (b, 0, h)),
        pl.BlockSpec((None, T, HD), lambda b, h: (b, 0, nh + h)),
        pl.BlockSpec((None, T, HD), lambda b, h: (b, 0, 2 * nh + h)),
        pl.BlockSpec((None, T, HD), lambda b, h: (b, 0, h)),
        pl.BlockSpec((1, 8, HD), lambda b, h: (h, 0, 0)),
        pl.BlockSpec((1, 8, HD), lambda b, h: (nh + h, 0, 0)),
        pl.BlockSpec((1, 8, HD), lambda b, h: (2 * nh + h, 0, 0)),
        pl.BlockSpec((GATE_ROWS, T), lambda b, h: (h, b)),
        pl.BlockSpec((1, HD), lambda b, h: (0, 0)),
    ]
    scratch = [
        pltpu.VMEM((T, HD), BF16), pltpu.VMEM((T, HD), BF16), pltpu.VMEM((T, HD), BF16),
        pltpu.VMEM((T, HD), F32),
        pltpu.VMEM((CONV_ROWS + 2 * CONV_HALO, HD), F32),
        pltpu.VMEM((CONV_ROWS + 2 * CONV_HALO, HD), F32),
        pltpu.VMEM((CONV_ROWS + 2 * CONV_HALO, HD), F32),
    ]
    per_direction = [pltpu.VMEM((2 * T, HD), BF16), pltpu.VMEM((T, HD), F32),
                     pltpu.VMEM((T // CHUNK * 8, HD), F32)]
    scratch = scratch + per_direction + per_direction
    return pl.pallas_call(
        functools.partial(_gdn_kernel, seq=T, prep_chunks=PREP_CHUNKS_SHORT if T <= SHORT_SEQ else PREP_CHUNKS),
        grid=(B, nh), in_specs=in_specs,
        out_specs=pl.BlockSpec((None, T, HD), lambda b, h: (b, 0, h)),
        out_shape=jax.ShapeDtypeStruct((B, T, nh * HD), BF16),
        scratch_shapes=scratch,
        compiler_params=pltpu.CompilerParams(dimension_semantics=("parallel", "parallel"),
                                             vmem_limit_bytes=VMEM_LIMIT),
        name="gdn",
    )(qkv, qkv, qkv, z, P["conv_w"], P["conv_w"], P["conv_w"], gates, P["gdn_nw"])


def _attn_kernel(q_ref, k_ref, vt_ref, o_ref, m_ref, acc_ref, sta_ref, stb_ref, *, seq, tk):
    tq = q_ref.shape[0]
    q = jnp.concatenate([q_ref[:, g * HD:(g + 1) * HD] for g in range(ATT_GROUP)], axis=0)
    m_ref[...] = jnp.full_like(m_ref, -jnp.inf)
    acc_ref[...] = jnp.zeros_like(acc_ref)
    ones = jnp.ones((ATT_ONES_ROWS, tk), BF16)
    n_kv = seq // tk

    def row0(j):
        return j * tk if isinstance(j, int) else pl.multiple_of(j * tk, tk)

    def scores(j, st_ref):
        st_ref[...] = _dot_nt(k_ref[pl.ds(row0(j), tk), :], q)

    def consume(j, st_ref):
        t0 = row0(j)
        m_old = m_ref[...]
        m_new = jnp.maximum(m_old, jnp.max(st_ref[...], axis=0, keepdims=True))
        pt = jnp.exp2(st_ref[...] - m_new).astype(BF16)
        alpha = jnp.exp2(m_old - m_new)
        vt = jnp.concatenate([vt_ref[:, pl.ds(t0, tk)], ones], axis=0)
        acc_ref[...] = alpha * acc_ref[...] + _dot(vt, pt)
        m_ref[...] = m_new

    scores(0, sta_ref)

    def body(i, carry):
        scores(2 * i + 1, stb_ref)
        consume(2 * i, sta_ref)
        scores(2 * i + 2, sta_ref)
        consume(2 * i + 1, stb_ref)
        return carry

    lax.fori_loop(0, n_kv // 2 - 1, body, 0)
    scores(n_kv - 1, stb_ref)
    consume(n_kv - 2, sta_ref)
    consume(n_kv - 1, stb_ref)
    acc = acc_ref[...]
    out = (acc[:HD] / acc[HD:HD + 1]).T
    for g in range(ATT_GROUP):
        o_ref[:, g * HD:(g + 1) * HD] = out[g * tq:(g + 1) * tq].astype(BF16)


def _attn(q, k, vt, B, T, tq, tk):
    gw = ATT_GROUP * HD
    return pl.pallas_call(
        functools.partial(_attn_kernel, seq=T, tk=tk),
        grid=(B, ATT_KV_HEADS, T // tq),
        in_specs=[
            pl.BlockSpec((None, tq, gw), lambda b, h, i: (b, i, h)),
            pl.BlockSpec((None, T, HD), lambda b, h, i: (b, 0, h)),
            pl.BlockSpec((HD, T), lambda b, h, i: (h, b)),
        ],
        out_specs=pl.BlockSpec((None, tq, gw), lambda b, h, i: (b, i, h)),
        out_shape=jax.ShapeDtypeStruct((B, T, ATT_HEADS * HD), BF16),
        scratch_shapes=[pltpu.VMEM((1, ATT_GROUP * tq), F32),
                        pltpu.VMEM((HD + ATT_ONES_ROWS, ATT_GROUP * tq), F32),
                        pltpu.VMEM((tk, ATT_GROUP * tq), F32), pltpu.VMEM((tk, ATT_GROUP * tq), F32)],
        compiler_params=pltpu.CompilerParams(dimension_semantics=("parallel", "parallel", "parallel"),
                                             vmem_limit_bytes=VMEM_LIMIT),
        name="attn",
    )(q, k, vt)


def _out_mlp_kernel(x_ref, oa_ref, ob_ref, woa_ref, wob_ref, npost_ref, npre2_ref, wup_ref, wdn_ref,
                    npost2_ref, y_ref, *, ff_chunk):
    mix = _dot(oa_ref[...], woa_ref[...]) + _dot(ob_ref[...], wob_ref[...])
    x1 = x_ref[...] + _rms(mix, npost_ref[...])
    hm = _rms(x1, npre2_ref[...]).astype(BF16)
    d_ff = wup_ref.shape[1]
    f = jnp.zeros(x1.shape, F32)
    for cc in range(d_ff // ff_chunk):
        hc = jnp.maximum(_dot(hm, wup_ref[:, cc * ff_chunk:(cc + 1) * ff_chunk]), 0.0)
        f = f + _dot((hc * hc).astype(BF16), wdn_ref[cc * ff_chunk:(cc + 1) * ff_chunk, :])
    y_ref[...] = x1 + _rms(f, npost2_ref[...])


def _out_mlp(x2, oa, ob, P, tm, ff_chunk):
    n_tok, d = x2.shape
    const = lambda i: (0, 0)
    tile = lambda i: (i, 0)
    single = pl.Buffered(1)

    def full(a):
        return pl.BlockSpec(a.shape, const, pipeline_mode=single)

    return pl.pallas_call(
        functools.partial(_out_mlp_kernel, ff_chunk=ff_chunk),
        grid=(n_tok // tm,),
        in_specs=[pl.BlockSpec((tm, d), tile), pl.BlockSpec((tm, oa.shape[1]), tile),
                  pl.BlockSpec((tm, ob.shape[1]), tile),
                  full(P["w_out_a"]), full(P["w_out_b"]), full(P["nw_post"]), full(P["nw_pre2"]),
                  full(P["w_up"]), full(P["w_down"]), full(P["nw_post2"])],
        out_specs=pl.BlockSpec((tm, d), tile),
        out_shape=jax.ShapeDtypeStruct((n_tok, d), F32),
        compiler_params=pltpu.CompilerParams(dimension_semantics=("parallel",), vmem_limit_bytes=VMEM_LIMIT),
        name="out_mlp",
    )(x2, oa, ob, P["w_out_a"], P["w_out_b"], P["nw_post"], P["nw_pre2"], P["w_up"], P["w_down"],
      P["nw_post2"])


def _rope_tables(T):
    n_freq = HD // 4
    t = np.arange(T)
    inv_freq = ROPE_THETA ** (-np.arange(n_freq, dtype=np.float32) / n_freq)
    row_ang = jnp.asarray((t // GRID_W).astype(np.float32))[:, None] * jnp.asarray(inv_freq)[None, :]
    col_ang = jnp.asarray((t % GRID_W).astype(np.float32))[:, None] * jnp.asarray(inv_freq)[None, :]
    cr, sr, cc, sc = jnp.cos(row_ang), jnp.sin(row_ang), jnp.cos(col_ang), jnp.sin(col_ang)
    return jnp.concatenate([cr, cc, cr, cc], axis=1), jnp.concatenate([-sr, -sc, sr, sc], axis=1)


def _chunk_cumsum_mats(tm):
    t = np.arange(tm)
    same = (t[:, None] // CHUNK) == (t[None, :] // CHUNK)
    fwd = same & (t[:, None] <= t[None, :])
    bwd = same & (t[:, None] >= t[None, :])
    return jnp.asarray(fwd, BF16), jnp.asarray(bwd, BF16)


def _prepare(norm_mix_pre, w_in, conv_w, A_log_f, A_log_b, dt_bias_f, dt_bias_b, gdn_norm_w, q_norm_w,
             k_norm_w, w_out, norm_mix_post, norm_mlp_pre, w_up, w_down, norm_mlp_post, seqs):
    gw = GDN_HEADS * HD
    sizes = (3 * gw, gw, GDN_HEADS, GDN_HEADS, GDN_HEADS, GDN_HEADS, ATT_HEADS * HD, ATT_KV_HEADS * HD,
             ATT_KV_HEADS * HD)
    offs = np.concatenate([[0], np.cumsum(sizes)])
    cols = [w_in[:, offs[i]:offs[i + 1]] for i in range(len(sizes))]
    w_qkv, w_z, w_bf, w_bb, w_af, w_ab, w_q, w_k, w_v = cols
    d = w_in.shape[0]
    perm = np.concatenate([np.arange(0, 32), np.arange(64, 96), np.arange(32, 64), np.arange(96, 128)])

    def permute_heads(w, n):
        return w.reshape(d, n, HD)[:, :, perm].reshape(d, n * HD)

    zeros = jnp.zeros((d, GDN_HEADS), F32)
    w_g = jnp.stack([w_bf, w_bb, w_af, w_ab, zeros, zeros, zeros, zeros], axis=2)
    w_g = w_g.reshape(d, GDN_HEADS * GATE_ROWS).T
    zh = jnp.zeros((GDN_HEADS,), F32)
    alog_rows = jnp.stack([zh, zh, A_log_f, A_log_b, zh, zh, zh, zh], axis=1).reshape(-1, 1)
    dtb_rows = jnp.stack([zh, zh, dt_bias_f, dt_bias_b, zh, zh, zh, zh], axis=1).reshape(-1, 1)
    cum_f, cum_b = _chunk_cumsum_mats(IN_SUB)
    cw = jnp.pad(conv_w, ((0, 8 - CONV_K), (0, 0)))
    cw = cw.reshape(8, 3 * GDN_HEADS, HD).transpose(1, 0, 2)
    row = lambda a: a.reshape(1, -1).astype(F32)
    return {
        "nw_pre": row(norm_mix_pre),
        "w_qkv": w_qkv.astype(BF16), "w_z": w_z.astype(BF16), "w_g": w_g.astype(BF16),
        "w_q": permute_heads(w_q, ATT_HEADS).astype(BF16),
        "w_k": permute_heads(w_k, ATT_KV_HEADS).astype(BF16),
        "w_v": w_v.T.astype(BF16),
        "alog_rows": alog_rows, "dtb_rows": dtb_rows, "cum_f": cum_f, "cum_b": cum_b,
        "rope": {T: _rope_tables(T) for T in seqs},
        "qnw": row(q_norm_w[perm]), "knw": row(k_norm_w[perm]),
        "conv_w": cw, "gdn_nw": row(gdn_norm_w),
        "w_out_a": w_out[:gw].astype(BF16), "w_out_b": w_out[gw:].astype(BF16),
        "nw_post": row(norm_mix_post), "nw_pre2": row(norm_mlp_pre),
        "w_up": w_up.astype(BF16), "w_down": w_down.astype(BF16), "nw_post2": row(norm_mlp_post),
    }


IN_TILE = 1024
MLP_TILE = 512
ATT_Q_TILE = 1024
ATT_K_TILE = 1024
FF_CHUNK = 1024


def _layer(x, P):
    B, T, d = x.shape
    x2 = x.reshape(B * T, d)
    qkv, z, gates, q, k, vt = _in_proj(x2, T, P, min(IN_TILE, T))
    o_a = _gdn(qkv.reshape(B, T, -1), z.reshape(B, T, -1), gates, P, B, T)
    o_b = _attn(q.reshape(B, T, -1), k.reshape(B, T, -1), vt, B, T, min(ATT_Q_TILE, T), min(ATT_K_TILE, T // 2))
    y = _out_mlp(x2, o_a.reshape(B * T, -1), o_b.reshape(B * T, -1), P, min(MLP_TILE, T), FF_CHUNK)
    return y.reshape(B, T, d)


def kernel(x_prompt, x_sample, norm_mix_pre, w_in, conv_w, A_log_f, A_log_b, dt_bias_f, dt_bias_b,
           gdn_norm_w, q_norm_w, k_norm_w, w_out, norm_mix_post, norm_mlp_pre, w_up, w_down,
           norm_mlp_post):
    y_prompt, y_sample = x_prompt, x_sample
    for l in range(w_in.shape[0]):
        P = _prepare(norm_mix_pre[l], w_in[l], conv_w[l], A_log_f[l], A_log_b[l], dt_bias_f[l],
                     dt_bias_b[l], gdn_norm_w[l], q_norm_w[l], k_norm_w[l], w_out[l], norm_mix_post[l],
                     norm_mlp_pre[l], w_up[l], w_down[l], norm_mlp_post[l],
                     seqs={y_prompt.shape[1], y_sample.shape[1]})
        y_prompt = _layer(y_prompt, P)
        y_sample = _layer(y_sample, P)
    return (y_prompt, y_sample)
```

```python
import functools
import math

import jax
import jax.numpy as jnp
import numpy as np
from jax import lax
from jax.experimental import pallas as pl
from jax.experimental.pallas import tpu as pltpu

F32 = jnp.float32
BF16 = jnp.bfloat16

EPS = 1e-6
GRID_W = 64
ROPE_THETA = 10000.0
CONV_K = 5
HD = 128
GDN_HEADS = 4
ATT_HEADS = 4
ATT_KV_HEADS = 2
ATT_GROUP = ATT_HEADS // ATT_KV_HEADS
LANES = 128
CHUNK = LANES
GATE_ROWS = 8
VMEM_LIMIT = 56 * 1024 * 1024
IN_SUB = 256
ATT_ONES_ROWS = 16

_NT = (((1,), (1,)), ((), ()))


def _dot(a, b):
    return jnp.dot(a, b, preferred_element_type=F32)


def _dot_nt(a, b):
    return lax.dot_general(a, b, _NT, preferred_element_type=F32)


def _rms(x, w):
    return x * lax.rsqrt(jnp.mean(x * x, axis=-1, keepdims=True) + EPS) * w


def _sigmoid(x):
    return 1.0 / (1.0 + jnp.exp(-x))


def _in_proj_kernel(x_ref, nw_ref, wqkv_ref, wz_ref, wg_ref, wq_ref, wk_ref, wv_ref,
                    alog_ref, dtb_ref, cumf_ref, cumb_ref, cos_ref, sin_ref, qnw_ref, knw_ref,
                    qkv_ref, z_ref, gate_ref, q_ref, k_ref, vt_ref, *, q_scale):
    for s in range(x_ref.shape[0] // IN_SUB):
        r = slice(s * IN_SUB, (s + 1) * IN_SUB)
        h = _rms(x_ref[r, :], nw_ref[...]).astype(BF16)

        qkv_ref[r, :] = _dot(h, wqkv_ref[...]).astype(BF16)
        z_ref[r, :] = _dot(h, wz_ref[...]).astype(BF16)
        vt_ref[:, r] = _dot_nt(wv_ref[...], h).astype(BF16)

        graw = _dot_nt(wg_ref[...], h)
        kind = lax.broadcasted_iota(jnp.int32, graw.shape, 0) % GATE_ROWS
        beta = _sigmoid(graw)
        a = graw + dtb_ref[...]
        softplus = jnp.maximum(a, 0.0) + jnp.log1p(jnp.exp(-jnp.abs(a)))
        g = -jnp.exp(alog_ref[...]) * softplus
        g = jnp.where(kind >= 2, g, 0.0)
        p1 = g.astype(BF16)
        r1 = g - p1.astype(F32)
        p2 = r1.astype(BF16)
        p3 = (r1 - p2.astype(F32)).astype(BF16)
        nrow = g.shape[0]
        parts = jnp.concatenate([p1, p2, p3], axis=0)
        cf = _dot(parts, cumf_ref[...])
        cb = _dot(parts, cumb_ref[...])
        cf = cf[:nrow] + cf[nrow:2 * nrow] + cf[2 * nrow:]
        cb = cb[:nrow] + cb[nrow:2 * nrow] + cb[2 * nrow:]
        gate_ref[:, r] = jnp.where(kind < 2, beta, jnp.where(kind == 2, cf, jnp.where(kind == 3, cb, 0.0)))

        cos = cos_ref[r, :]
        sin = sin_ref[r, :]

        def norm_rope(y, w, scale):
            y = _rms(y, w)
            y = y * cos + pltpu.roll(y, HD // 2, axis=1) * sin
            return (y * scale).astype(BF16) if scale != 1.0 else y.astype(BF16)

        qa = _dot(h, wq_ref[...])
        for hh in range(ATT_HEADS):
            q_ref[r, hh * HD:(hh + 1) * HD] = norm_rope(qa[:, hh * HD:(hh + 1) * HD], qnw_ref[...], q_scale)
        ka = _dot(h, wk_ref[...])
        for hh in range(ATT_KV_HEADS):
            k_ref[r, hh * HD:(hh + 1) * HD] = norm_rope(ka[:, hh * HD:(hh + 1) * HD], knw_ref[...], 1.0)


def _in_proj(x2, T, P, tm):
    n_tok, d = x2.shape
    nt = n_tok // tm
    tpb = T // tm
    const = lambda i: (0, 0)
    tile = lambda i: (i, 0)

    def full(a):
        return pl.BlockSpec(a.shape, const)

    in_specs = [
        pl.BlockSpec((tm, d), tile), full(P["nw_pre"]),
        full(P["w_qkv"]), full(P["w_z"]), full(P["w_g"]), full(P["w_q"]), full(P["w_k"]), full(P["w_v"]),
        full(P["alog_rows"]), full(P["dtb_rows"]), full(P["cum_f"]), full(P["cum_b"]),
        pl.BlockSpec((tm, HD), lambda i: (i % tpb, 0)), pl.BlockSpec((tm, HD), lambda i: (i % tpb, 0)),
        full(P["qnw"]), full(P["knw"]),
    ]
    n_gate = GDN_HEADS * GATE_ROWS
    out_shape = [
        jax.ShapeDtypeStruct((n_tok, 3 * GDN_HEADS * HD), BF16),
        jax.ShapeDtypeStruct((n_tok, GDN_HEADS * HD), BF16),
        jax.ShapeDtypeStruct((n_gate, n_tok), F32),
        jax.ShapeDtypeStruct((n_tok, ATT_HEADS * HD), BF16),
        jax.ShapeDtypeStruct((n_tok, ATT_KV_HEADS * HD), BF16),
        jax.ShapeDtypeStruct((ATT_KV_HEADS * HD, n_tok), BF16),
    ]
    out_specs = [
        pl.BlockSpec((tm, 3 * GDN_HEADS * HD), tile),
        pl.BlockSpec((tm, GDN_HEADS * HD), tile),
        pl.BlockSpec((n_gate, tm), lambda i: (0, i)),
        pl.BlockSpec((tm, ATT_HEADS * HD), tile),
        pl.BlockSpec((tm, ATT_KV_HEADS * HD), tile),
        pl.BlockSpec((ATT_KV_HEADS * HD, tm), lambda i: (0, i)),
    ]
    q_scale = (HD ** -0.5) * math.log2(math.e)
    cos, sin = P["rope"][T]
    return pl.pallas_call(
        functools.partial(_in_proj_kernel, q_scale=q_scale),
        grid=(nt,), in_specs=in_specs, out_specs=out_specs, out_shape=out_shape,
        compiler_params=pltpu.CompilerParams(dimension_semantics=("parallel",), vmem_limit_bytes=VMEM_LIMIT),
        name="in_proj",
    )(x2, P["nw_pre"], P["w_qkv"], P["w_z"], P["w_g"], P["w_q"], P["w_k"], P["w_v"],
      P["alog_rows"], P["dtb_rows"], P["cum_f"], P["cum_b"], cos, sin, P["qnw"], P["knw"])


PREP_CHUNKS = 8
CONV_ROWS = 256
CONV_HALO = 16


def _conv_silu(x_ref, cw, win_ref, t0, seq):
    r, hl = CONV_ROWS, CONV_HALO
    prev = x_ref[pl.ds(pl.multiple_of(jnp.maximum(t0 - hl, 0), hl), hl), :].astype(F32)
    nxt = x_ref[pl.ds(pl.multiple_of(jnp.minimum(t0 + r, seq - hl), hl), hl), :].astype(F32)
    win_ref[0:hl, :] = jnp.where(t0 > 0, prev, 0.0)
    win_ref[hl:hl + r, :] = x_ref[pl.ds(t0, r), :].astype(F32)
    win_ref[hl + r:hl + r + hl, :] = jnp.where(t0 + r < seq, nxt, 0.0)
    first = hl - CONV_K // 2
    acc = cw[0:1, :] * win_ref[first:first + r, :]
    for j in range(1, CONV_K):
        acc = acc + cw[j:j + 1, :] * win_ref[first + j:first + j + r, :]
    return acc * _sigmoid(acc)


def _l2n(x, scale):
    return x * (lax.rsqrt(jnp.sum(x * x, axis=-1, keepdims=True) + EPS) * scale)


def _gdn_kernel(q_ref, k_ref, v_ref, z_ref, cwq_ref, cwk_ref, cwv_ref, gate_ref, nw_ref, o_ref,
                qn_ref, kn_ref, vn_ref, ob_ref, winq_ref, wink_ref, winv_ref,
                mqf_ref, nf_ref, decf_ref, mqb_ref, nb_ref, decb_ref, *, seq):
    c = CHUNK
    group_rows = PREP_CHUNKS * c
    dir_refs = ((mqf_ref, nf_ref, decf_ref), (mqb_ref, nb_ref, decb_ref))

    cwq, cwk, cwv = cwq_ref[0], cwk_ref[0], cwv_ref[0]
    n_group = seq // group_rows

    def conv_pieces(g):
        def piece(x_ref, cw, win_ref, dst_ref, normalise, scale, half):
            def run():
                t0 = pl.multiple_of(g * group_rows + half * CONV_ROWS, CONV_ROWS)
                y = _conv_silu(x_ref, cw, win_ref, t0, seq)
                if normalise:
                    y = _l2n(y, scale)
                dst_ref[pl.ds(t0, CONV_ROWS), :] = y.astype(BF16)
            return run
        streams = ((q_ref, cwq, winq_ref, qn_ref, True, HD ** -0.5), (k_ref, cwk, wink_ref, kn_ref, True, 1.0),
                   (v_ref, cwv, winv_ref, vn_ref, False, 1.0))
        return [piece(*s, half) for s in streams for half in range(group_rows // CONV_ROWS)]

    ri = lax.broadcasted_iota(jnp.int32, (c, c), 0)
    ci = lax.broadcasted_iota(jnp.int32, (c, c), 1)
    eye = ri == ci

    def col(row):
        return jnp.sum(jnp.where(eye, row, 0.0), axis=1, keepdims=True)

    def inv_unit_tri(mats, fillers):
        ts = [jnp.where(eye, 1.0, 0.0) - jnp.where((ri // 2) == (ci // 2), a, 0.0) for a in mats]
        fillers = list(fillers)
        per_level = -(-len(fillers) // (c.bit_length() - 2))
        b = 2
        while b < c:
            off = ((ri // (2 * b)) == (ci // (2 * b))) & ((ri // b) != (ci // b))
            tbs = [t.astype(BF16) for t in ts]
            xs = [_dot(jnp.where(off, a, 0.0).astype(BF16), tb).astype(BF16) for a, tb in zip(mats, tbs)]
            for f in fillers[:per_level]:
                f()
            fillers = fillers[per_level:]
            ts = [t - _dot(tb, x) for t, tb, x in zip(ts, tbs, xs)]
            b *= 2
        return ts

    def prep_group(probs, first, fillers):
        t0s = [pl.multiple_of(ch * c, c) for ch, d, r in probs]
        qs = [qn_ref[pl.ds(t0, c), :] for t0 in t0s]
        ks = [kn_ref[pl.ds(t0, c), :] for t0 in t0s]
        vs = [vn_ref[pl.ds(t0, c), :] for t0 in t0s]
        gts = [gate_ref[:, pl.ds(t0, c)] for t0 in t0s]
        kqk = [_dot_nt(jnp.concatenate([k, q], axis=0), k) for q, k in zip(qs, ks)]
        grams = [x[:c] for x in kqk]
        qks = [x[c:] for x in kqk]
        kts = [k.astype(F32).T for k in ks]
        beta_r = [g[d:d + 1] for g, (ch, d, r) in zip(gts, probs)]
        gc_r = [g[2 + d:3 + d] for g, (ch, d, r) in zip(gts, probs)]
        gc_c = [col(g) for g in gc_r]
        beta_c = [col(bb) for bb in beta_r]
        masks = [(ri <= ci) if d else (ri >= ci) for ch, d, r in probs]
        stricts = [(ri < ci) if d else (ri > ci) for ch, d, r in probs]
        es = [jnp.where(m, jnp.exp(jnp.where(m, gcc - gcr, 0.0)), 0.0) for m, gcc, gcr in zip(masks, gc_c, gc_r)]
        amats = [jnp.where(st, gram * e * bc, 0.0) for gram, st, e, bc in zip(grams, stricts, es, beta_c)]
        ts = inv_unit_tri(amats, fillers)
        us = [_dot((t * br).astype(BF16), v) for t, br, v in zip(ts, beta_r, vs)]
        ws = [_dot((t * (br * jnp.exp(gr))).astype(BF16), k) for t, br, gr, k in zip(ts, beta_r, gc_r, ks)]
        wus = [jnp.concatenate([w, u], axis=1).astype(BF16) for w, u in zip(ws, us)]
        g_last = [gr[:, 0:1] if d else gr[:, c - 1:c] for (ch, d, r), gr in zip(probs, gc_r)]
        kes = [(kt * jnp.exp(gl - gr)).astype(BF16) for kt, gl, gr in zip(kts, g_last, gc_r)]
        qkms = [(qk * e).astype(BF16) for qk, e in zip(qks, es)]
        mq = [_dot(jnp.concatenate([ke, qkm], axis=0), wu) for ke, qkm, wu in zip(kes, qkms, wus)]
        for i, (ch, d, r) in enumerate(probs):
            mn, qo = mq[i][:c], mq[i][c:]
            mq_ref, n_ref, dec_ref = dir_refs[d]
            m0 = pl.multiple_of(r * (2 * c), 2 * c)
            mq_ref[pl.ds(m0, c), :] = mn[:, :HD].astype(BF16)
            q_dec = qs[i].astype(F32) * jnp.exp(gc_c[i])
            mq_ref[pl.ds(m0 + c, c), :] = (q_dec - qo[:, :HD]).astype(BF16)
            n_ref[pl.ds(pl.multiple_of(r * c, c), c), :] = mn[:, HD:]
            dec_ref[pl.ds(pl.multiple_of(r * 8, 8), 8), :] = jnp.broadcast_to(jnp.exp(g_last[i]), (8, HD))
            if first:
                ob_ref[pl.ds(t0s[i], c), :] = qo[:, HD:]
            else:
                ob_ref[pl.ds(t0s[i], c), :] += qo[:, HD:]

    def step(chunk, r, s, refs):
        mq_ref, n_ref, dec_ref = refs
        t0 = pl.multiple_of(chunk * c, c)
        m0 = pl.multiple_of(r * (2 * c), 2 * c)
        s_bf = s.astype(BF16)
        ms = _dot(mq_ref[pl.ds(m0, c), :], s_bf)
        dec = dec_ref[pl.ds(pl.multiple_of(r * 8, 8), 8), :][0:1]
        ob_ref[pl.ds(t0, c), :] += _dot(mq_ref[pl.ds(m0 + c, c), :], s_bf)
        return s * dec - ms + n_ref[pl.ds(pl.multiple_of(r * c, c), c), :]

    half = n_group // 2

    def group_probs(g, d):
        grp = (n_group - 1 - g) if d else g
        return [(grp * PREP_CHUNKS + n, d, (g % 2) * PREP_CHUNKS + n) for n in range(PREP_CHUNKS)]

    def scan_steps(g, state):
        def one(chunk, d, r):
            def run():
                state[d] = step(chunk, r, state[d], dir_refs[d])
            return run
        pairs = zip(group_probs(g, 0), group_probs(g, 1)[::-1])
        return [one(*p) for pf, pb in pairs for p in (pf, pb)]

    def next_convs(g):
        return conv_pieces(g + 1) + conv_pieces(n_group - 2 - g)

    def iteration(g, first, fillers):
        prep_group(group_probs(g, 0) + group_probs(g, 1), first, fillers)

    for run in conv_pieces(0) + conv_pieces(n_group - 1):
        run()
    iteration(0, True, next_convs(0))

    def first_half(g, carry):
        state = list(carry)
        iteration(g, True, scan_steps(g - 1, state) + next_convs(g))
        return tuple(state)

    nw = nw_ref[...]

    def norm_pieces(g):
        def piece(half_idx):
            def run():
                t0 = pl.multiple_of(g * group_rows + half_idx * CONV_ROWS, CONV_ROWS)
                o = _rms(ob_ref[pl.ds(t0, CONV_ROWS), :], nw)
                zz = z_ref[pl.ds(t0, CONV_ROWS), :].astype(F32)
                o_ref[pl.ds(t0, CONV_ROWS), :] = (o * (zz * _sigmoid(zz))).astype(BF16)
            return run
        return [piece(i) for i in range(group_rows // CONV_ROWS)]

    def second_half(g, carry):
        state = list(carry)
        iteration(g, False, scan_steps(g - 1, state))
        return tuple(state)

    def second_half_norm(g, carry):
        state = list(carry)
        iteration(g, False, scan_steps(g - 1, state) + norm_pieces(g - 2) + norm_pieces(n_group + 1 - g))
        return tuple(state)

    s0 = jnp.zeros((HD, HD), F32)
    norm_from = min(half + 2, n_group)
    carry = lax.fori_loop(1, half, first_half, (s0, s0))
    carry = lax.fori_loop(half, norm_from, second_half, carry)
    carry = lax.fori_loop(norm_from, n_group, second_half_norm, carry)
    state = list(carry)
    for run in scan_steps(n_group - 1, state):
        run()
    normed = {grp for g in range(norm_from, n_group) for grp in (g - 2, n_group + 1 - g)}
    for g in sorted(set(range(n_group)) - normed):
        for run in norm_pieces(g):
            run()


def _gdn(qkv, z, gates, P, B, T):
    nh = GDN_HEADS
    in_specs = [
        pl.BlockSpec((None, T, HD), lambda b, h: (b, 0, h)),
        pl.BlockSpec((None, T, HD), lambda b, h: (b, 0, nh + h)),
        pl.BlockSpec((None, T, HD), lambda b, h: (b, 0, 2 * nh + h)),
        pl.BlockSpec((None, T, HD), lambda b, h: (b, 0, h)),
        pl.BlockSpec((1, 8, HD), lambda b, h: (h, 0, 0)),
        pl.BlockSpec((1, 8, HD), lambda b, h: (nh + h, 0, 0)),
        pl.BlockSpec((1, 8, HD), lambda b, h: (2 * nh + h, 0, 0)),
        pl.BlockSpec((GATE_ROWS, T), lambda b, h: (h, b)),
        pl.BlockSpec((1, HD), lambda b, h: (0, 0)),
    ]
    scratch = [
        pltpu.VMEM((T, HD), BF16), pltpu.VMEM((T, HD), BF16), pltpu.VMEM((T, HD), BF16),
        pltpu.VMEM((T, HD), F32),
        pltpu.VMEM((CONV_ROWS + 2 * CONV_HALO, HD), F32),
        pltpu.VMEM((CONV_ROWS + 2 * CONV_HALO, HD), F32),
        pltpu.VMEM((CONV_ROWS + 2 * CONV_HALO, HD), F32),
    ]
    ring = 2 * PREP_CHUNKS
    per_direction = [pltpu.VMEM((ring * 2 * CHUNK, HD), BF16), pltpu.VMEM((ring * CHUNK, HD), F32),
                     pltpu.VMEM((ring * 8, HD), F32)]
    scratch = scratch + per_direction + per_direction
    return pl.pallas_call(
        functools.partial(_gdn_kernel, seq=T),
        grid=(B, nh), in_specs=in_specs,
        out_specs=pl.BlockSpec((None, T, HD), lambda b, h: (b, 0, h)),
        out_shape=jax.ShapeDtypeStruct((B, T, nh * HD), BF16),
        scratch_shapes=scratch,
        compiler_params=pltpu.CompilerParams(dimension_semantics=("parallel", "parallel"),
                                             vmem_limit_bytes=VMEM_LIMIT),
        name="gdn",
    )(qkv, qkv, qkv, z, P["conv_w"], P["conv_w"], P["conv_w"], gates, P["gdn_nw"])


def _attn_kernel(q_ref, k_ref, vt_ref, o_ref, m_ref, acc_ref, sta_ref, stb_ref, *, seq, tk):
    tq = q_ref.shape[0]
    q = jnp.concatenate([q_ref[:, g * HD:(g + 1) * HD] for g in range(ATT_GROUP)], axis=0)
    m_ref[...] = jnp.full_like(m_ref, -jnp.inf)
    acc_ref[...] = jnp.zeros_like(acc_ref)
    ones = jnp.ones((ATT_ONES_ROWS, tk), BF16)
    n_kv = seq // tk

    def row0(j):
        return j * tk if isinstance(j, int) else pl.multiple_of(j * tk, tk)

    def scores(j, st_ref):
        st_ref[...] = _dot_nt(k_ref[pl.ds(row0(j), tk), :], q)

    def consume(j, st_ref):
        t0 = row0(j)
        m_old = m_ref[...]
        m_new = jnp.maximum(m_old, jnp.max(st_ref[...], axis=0, keepdims=True))
        pt = jnp.exp2(st_ref[...] - m_new).astype(BF16)
        alpha = jnp.exp2(m_old - m_new)
        vt = jnp.concatenate([vt_ref[:, pl.ds(t0, tk)], ones], axis=0)
        acc_ref[...] = alpha * acc_ref[...] + _dot(vt, pt)
        m_ref[...] = m_new

    scores(0, sta_ref)

    def body(i, carry):
        scores(2 * i + 1, stb_ref)
        consume(2 * i, sta_ref)
        scores(2 * i + 2, sta_ref)
        consume(2 * i + 1, stb_ref)
        return carry

    lax.fori_loop(0, n_kv // 2 - 1, body, 0)
    scores(n_kv - 1, stb_ref)
    consume(n_kv - 2, sta_ref)
    consume(n_kv - 1, stb_ref)
    acc = acc_ref[...]
    out = (acc[:HD] / acc[HD:HD + 1]).T
    for g in range(ATT_GROUP):
        o_ref[:, g * HD:(g + 1) * HD] = out[g * tq:(g + 1) * tq].astype(BF16)


def _attn(q, k, vt, B, T, tq, tk):
    gw = ATT_GROUP * HD
    return pl.pallas_call(
        functools.partial(_attn_kernel, seq=T, tk=tk),
        grid=(B, ATT_KV_HEADS, T // tq),
        in_specs=[
            pl.BlockSpec((None, tq, gw), lambda b, h, i: (b, i, h)),
            pl.BlockSpec((None, T, HD), lambda b, h, i: (b, 0, h)),
            pl.BlockSpec((HD, T), lambda b, h, i: (h, b)),
        ],
        out_specs=pl.BlockSpec((None, tq, gw), lambda b, h, i: (b, i, h)),
        out_shape=jax.ShapeDtypeStruct((B, T, ATT_HEADS * HD), BF16),
        scratch_shapes=[pltpu.VMEM((1, ATT_GROUP * tq), F32),
                        pltpu.VMEM((HD + ATT_ONES_ROWS, ATT_GROUP * tq), F32),
                        pltpu.VMEM((tk, ATT_GROUP * tq), F32), pltpu.VMEM((tk, ATT_GROUP * tq), F32)],
        compiler_params=pltpu.CompilerParams(dimension_semantics=("parallel", "parallel", "parallel"),
                                             vmem_limit_bytes=VMEM_LIMIT),
        name="attn",
    )(q, k, vt)


def _out_mlp_kernel(x_ref, oa_ref, ob_ref, woa_ref, wob_ref, npost_ref, npre2_ref, wup_ref, wdn_ref,
                    npost2_ref, y_ref, *, ff_chunk):
    mix = _dot(oa_ref[...], woa_ref[...]) + _dot(ob_ref[...], wob_ref[...])
    x1 = x_ref[...] + _rms(mix, npost_ref[...])
    hm = _rms(x1, npre2_ref[...]).astype(BF16)
    d_ff = wup_ref.shape[1]
    f = jnp.zeros(x1.shape, F32)
    for cc in range(d_ff // ff_chunk):
        hc = jnp.maximum(_dot(hm, wup_ref[:, cc * ff_chunk:(cc + 1) * ff_chunk]), 0.0)
        f = f + _dot((hc * hc).astype(BF16), wdn_ref[cc * ff_chunk:(cc + 1) * ff_chunk, :])
    y_ref[...] = x1 + _rms(f, npost2_ref[...])


def _out_mlp(x2, oa, ob, P, tm, ff_chunk):
    n_tok, d = x2.shape
    const = lambda i: (0, 0)
    tile = lambda i: (i, 0)
    single = pl.Buffered(1)

    def full(a):
        return pl.BlockSpec(a.shape, const, pipeline_mode=single)

    return pl.pallas_call(
        functools.partial(_out_mlp_kernel, ff_chunk=ff_chunk),
        grid=(n_tok // tm,),
        in_specs=[pl.BlockSpec((tm, d), tile), pl.BlockSpec((tm, oa.shape[1]), tile),
                  pl.BlockSpec((tm, ob.shape[1]), tile),
                  full(P["w_out_a"]), full(P["w_out_b"]), full(P["nw_post"]), full(P["nw_pre2"]),
                  full(P["w_up"]), full(P["w_down"]), full(P["nw_post2"])],
        out_specs=pl.BlockSpec((tm, d), tile),
        out_shape=jax.ShapeDtypeStruct((n_tok, d), F32),
        compiler_params=pltpu.CompilerParams(dimension_semantics=("parallel",), vmem_limit_bytes=VMEM_LIMIT),
        name="out_mlp",
    )(x2, oa, ob, P["w_out_a"], P["w_out_b"], P["nw_post"], P["nw_pre2"], P["w_up"], P["w_down"],
      P["nw_post2"])


def _rope_tables(T):
    n_freq = HD // 4
    t = np.arange(T)
    inv_freq = ROPE_THETA ** (-np.arange(n_freq, dtype=np.float32) / n_freq)
    row_ang = jnp.asarray((t // GRID_W).astype(np.float32))[:, None] * jnp.asarray(inv_freq)[None, :]
    col_ang = jnp.asarray((t % GRID_W).astype(np.float32))[:, None] * jnp.asarray(inv_freq)[None, :]
    cr, sr, cc, sc = jnp.cos(row_ang), jnp.sin(row_ang), jnp.cos(col_ang), jnp.sin(col_ang)
    return jnp.concatenate([cr, cc, cr, cc], axis=1), jnp.concatenate([-sr, -sc, sr, sc], axis=1)


def _chunk_cumsum_mats(tm):
    t = np.arange(tm)
    same = (t[:, None] // CHUNK) == (t[None, :] // CHUNK)
    fwd = same & (t[:, None] <= t[None, :])
    bwd = same & (t[:, None] >= t[None, :])
    return jnp.asarray(fwd, BF16), jnp.asarray(bwd, BF16)


def _prepare(norm_mix_pre, w_in, conv_w, A_log_f, A_log_b, dt_bias_f, dt_bias_b, gdn_norm_w, q_norm_w,
             k_norm_w, w_out, norm_mix_post, norm_mlp_pre, w_up, w_down, norm_mlp_post, seqs):
    gw = GDN_HEADS * HD
    sizes = (3 * gw, gw, GDN_HEADS, GDN_HEADS, GDN_HEADS, GDN_HEADS, ATT_HEADS * HD, ATT_KV_HEADS * HD,
             ATT_KV_HEADS * HD)
    offs = np.concatenate([[0], np.cumsum(sizes)])
    cols = [w_in[:, offs[i]:offs[i + 1]] for i in range(len(sizes))]
    w_qkv, w_z, w_bf, w_bb, w_af, w_ab, w_q, w_k, w_v = cols
    d = w_in.shape[0]
    perm = np.concatenate([np.arange(0, 32), np.arange(64, 96), np.arange(32, 64), np.arange(96, 128)])

    def permute_heads(w, n):
        return w.reshape(d, n, HD)[:, :, perm].reshape(d, n * HD)

    zeros = jnp.zeros((d, GDN_HEADS), F32)
    w_g = jnp.stack([w_bf, w_bb, w_af, w_ab, zeros, zeros, zeros, zeros], axis=2)
    w_g = w_g.reshape(d, GDN_HEADS * GATE_ROWS).T
    zh = jnp.zeros((GDN_HEADS,), F32)
    alog_rows = jnp.stack([zh, zh, A_log_f, A_log_b, zh, zh, zh, zh], axis=1).reshape(-1, 1)
    dtb_rows = jnp.stack([zh, zh, dt_bias_f, dt_bias_b, zh, zh, zh, zh], axis=1).reshape(-1, 1)
    cum_f, cum_b = _chunk_cumsum_mats(IN_SUB)
    cw = jnp.pad(conv_w, ((0, 8 - CONV_K), (0, 0)))
    cw = cw.reshape(8, 3 * GDN_HEADS, HD).transpose(1, 0, 2)
    row = lambda a: a.reshape(1, -1).astype(F32)
    return {
        "nw_pre": row(norm_mix_pre),
        "w_qkv": w_qkv.astype(BF16), "w_z": w_z.astype(BF16), "w_g": w_g.astype(BF16),
        "w_q": permute_heads(w_q, ATT_HEADS).astype(BF16),
        "w_k": permute_heads(w_k, ATT_KV_HEADS).astype(BF16),
        "w_v": w_v.T.astype(BF16),
        "alog_rows": alog_rows, "dtb_rows": dtb_rows, "cum_f": cum_f, "cum_b": cum_b,
        "rope": {T: _rope_tables(T) for T in seqs},
        "qnw": row(q_norm_w[perm]), "knw": row(k_norm_w[perm]),
        "conv_w": cw, "gdn_nw": row(gdn_norm_w),
        "w_out_a": w_out[:gw].astype(BF16), "w_out_b": w_out[gw:].astype(BF16),
        "nw_post": row(norm_mix_post), "nw_pre2": row(norm_mlp_pre),
        "w_up": w_up.astype(BF16), "w_down": w_down.astype(BF16), "nw_post2": row(norm_mlp_post),
    }


IN_TILE = 1024
MLP_TILE = 512
ATT_Q_TILE = 1024
ATT_K_TILE = 1024
FF_CHUNK = 1024


def _layer(x, P):
    B, T, d = x.shape
    x2 = x.reshape(B * T, d)
    qkv, z, gates, q, k, vt = _in_proj(x2, T, P, min(IN_TILE, T))
    o_a = _gdn(qkv.reshape(B, T, -1), z.reshape(B, T, -1), gates, P, B, T)
    o_b = _attn(q.reshape(B, T, -1), k.reshape(B, T, -1), vt, B, T, min(ATT_Q_TILE, T), min(ATT_K_TILE, T // 2))
    y = _out_mlp(x2, o_a.reshape(B * T, -1), o_b.reshape(B * T, -1), P, min(MLP_TILE, T), FF_CHUNK)
    return y.reshape(B, T, d)


def kernel(x_prompt, x_sample, norm_mix_pre, w_in, conv_w, A_log_f, A_log_b, dt_bias_f, dt_bias_b,
           gdn_norm_w, q_norm_w, k_norm_w, w_out, norm_mix_post, norm_mlp_pre, w_up, w_down,
           norm_mlp_post):
    y_prompt, y_sample = x_prompt, x_sample
    for l in range(w_in.shape[0]):
        P = _prepare(norm_mix_pre[l], w_in[l], conv_w[l], A_log_f[l], A_log_b[l], dt_bias_f[l],
                     dt_bias_b[l], gdn_norm_w[l], q_norm_w[l], k_norm_w[l], w_out[l], norm_mix_post[l],
                     norm_mlp_pre[l], w_up[l], w_down[l], norm_mlp_post[l],
                     seqs={y_prompt.shape[1], y_sample.shape[1]})
        y_prompt = _layer(y_prompt, P)
        y_sample = _layer(y_sample, P)
    return (y_prompt, y_sample)
```
